```python
import math
import jax
import jax.numpy as jnp
from jax import lax
import numpy as np

D_MODEL = 2048
BATCH = 8
SEQ = 4096
DEPTH = 2

D_MIX = D_MODEL
RET_HEADS = 6
RET_WIDTH = 3 * D_MIX // 8
RET_HEAD_DIM = RET_WIDTH // RET_HEADS
RET_CHUNK = 128
ROPE_BASE = 10000.0
DIFF_HEADS = 6
DIFF_WIDTH = 3 * D_MIX // 8
DIFF_V_DIM = DIFF_WIDTH // DIFF_HEADS
DIFF_QK_DIM = DIFF_V_DIM // 2
ATTN_BLOCK = 128
GMLP_GROUPS = 4
GMLP_WIDTH = D_MIX - RET_WIDTH - DIFF_WIDTH
GMLP_GROUP_DIM = GMLP_WIDTH // GMLP_GROUPS
GMLP_CHUNK = 128
IN_WIDTHS = (RET_WIDTH,) * 4 + (DIFF_WIDTH,) * 3 + (GMLP_WIDTH,) * 2
IN_SPLITS = tuple(int(s) for s in np.cumsum(IN_WIDTHS)[:-1])
D_IN = sum(IN_WIDTHS)
N_GROUPS = 4
EXPERTS_PER_GROUP = 8
N_EXPERTS = N_GROUPS * EXPERTS_PER_GROUP
TOP_K = 2
EXPERT_HIDDEN = D_MODEL // 2
MOE_BLOCK = 128
N_MOD = 6
EPS = 1e-6

kernel_name = 'hybrid_retention_diffattn_gmlp_hmoe'


def _rmsnorm(x, w):
    xf = x.astype(jnp.float32)
    y = xf * lax.rsqrt(jnp.mean(xf * xf, axis=-1, keepdims=True) + EPS)
    return (y * w.astype(jnp.float32)).astype(x.dtype)


def _layernorm(x, w, b=None):
    xf = x.astype(jnp.float32)
    mu = jnp.mean(xf, axis=-1, keepdims=True)
    xc = xf - mu
    y = xc * lax.rsqrt(jnp.mean(xc * xc, axis=-1, keepdims=True) + EPS) * w.astype(jnp.float32)
    if b is not None:
        y = y + b.astype(jnp.float32)
    return y.astype(x.dtype)


def _rotary(x, pos):
    half = x.shape[-1] // 2
    inv = ROPE_BASE ** (-jnp.arange(half, dtype=jnp.float32) / half)
    ang = pos.astype(jnp.float32)[:, None] * inv[None, :]
    cos = jnp.cos(ang)[None, :, None, :]
    sin = jnp.sin(ang)[None, :, None, :]
    xf = x.astype(jnp.float32)
    x1, x2 = xf[..., :half], xf[..., half:]
    return jnp.concatenate([x1 * cos - x2 * sin, x2 * cos + x1 * sin], axis=-1).astype(x.dtype)


def _chunkwise_retention(q, k, v):
    B, S, H, dk = q.shape
    dv = v.shape[-1]
    C = RET_CHUNK
    NC = S // C
    dt = q.dtype

    def to_chunks(t):
        return t.reshape(B, NC, C, H, t.shape[-1]).transpose(0, 3, 1, 2, 4)

    qc, kc, vc = to_chunks(q), to_chunks(k), to_chunks(v)
    log_g = jnp.log(1.0 - jnp.power(2.0, -5.0 - jnp.arange(H, dtype=jnp.float32)))
    idx = jnp.arange(C, dtype=jnp.float32)
    rel = idx[:, None] - idx[None, :]
    decay = jnp.where(rel[None] >= 0,
                      jnp.exp(jnp.maximum(rel, 0.0)[None] * log_g[:, None, None]),
                      0.0).astype(dt)
    scores = jnp.einsum('bhncd,bhnmd->bhncm', qc, kc) * decay[None, :, None]
    y_inner = jnp.einsum('bhncm,bhnme->bhnce', scores, vc)
    zeta = jnp.exp((C - 1.0 - idx)[None, :] * log_g[:, None]).astype(dt)
    kv = jnp.einsum('bhncd,bhnce->nbhde', kc * zeta[None, :, None, :, None], vc)
    chunk_decay = jnp.exp(C * log_g).astype(dt)[None, :, None, None]

    def step(state, kv_n):
        return state * chunk_decay + kv_n, state

    _, states = lax.scan(step, jnp.zeros((B, H, dk, dv), dt), kv)
    xi = jnp.exp((idx + 1.0)[None, :] * log_g[:, None]).astype(dt)
    y_cross = jnp.einsum('bhncd,nbhde->bhnce', qc, states) * xi[None, :, None, :, None]
    y = y_inner + y_cross
    return y.transpose(0, 2, 3, 1, 4).reshape(B, S, H, dv)


def _retention(q, k, v, g, norm_w):
    B, S, _ = q.shape
    H, d = RET_HEADS, RET_HEAD_DIM
    pos = jnp.arange(S)
    q = _rotary(q.reshape(B, S, H, d), pos)
    k = _rotary(k.reshape(B, S, H, d), pos) * (d ** -0.5)
    v = v.reshape(B, S, H, d)
    y = _chunkwise_retention(q, k, v)
    y = _layernorm(y, norm_w.reshape(H, d))
    return jax.nn.silu(g) * y.reshape(B, S, H * d)


def _diff_attention(q, k, v, q_norm_w, k_norm_w, lam_params, subln_w, lam_init):
    B, S, _ = q.shape
    H, dh, dv = DIFF_HEADS, DIFF_QK_DIM, DIFF_V_DIM
    q = _rmsnorm(q.reshape(B, S, H, 2, dh), q_norm_w).transpose(0, 2, 3, 1, 4)
    k = _rmsnorm(k.reshape(B, S, H, 2, dh), k_norm_w).transpose(0, 2, 3, 1, 4)
    v = v.reshape(B, S, H, dv).transpose(0, 2, 1, 3)
    lp = lam_params.astype(jnp.float32)
    lam = jnp.exp(jnp.sum(lp[0] * lp[1])) - jnp.exp(jnp.sum(lp[2] * lp[3])) + lam_init
    scale = dh ** -0.5
    outs = []
    for i in range(S // ATTN_BLOCK):
        q0 = i * ATTN_BLOCK
        L = q0 + ATTN_BLOCK
        s = jnp.einsum('bhjqd,bhjkd->bhjqk', q[:, :, :, q0:L], k[:, :, :, :L]).astype(jnp.float32) * scale
        mask = jnp.arange(L)[None, :] <= (q0 + jnp.arange(ATTN_BLOCK))[:, None]
        p = jax.nn.softmax(jnp.where(mask, s, -jnp.inf), axis=-1)
        a = (p[:, :, 0] - lam * p[:, :, 1]).astype(v.dtype)
        outs.append(jnp.einsum('bhqk,bhke->bhqe', a, v[:, :, :L]))
    o = jnp.concatenate(outs, axis=2)
    o = _rmsnorm(o, subln_w) * (1.0 - lam_init)
    return o.transpose(0, 2, 1, 3).reshape(B, S, H * dv)


def _gmlp(u, v, norm_w, norm_b, ws, bs):
    B, S, _ = u.shape
    G, cg, C = GMLP_GROUPS, GMLP_GROUP_DIM, GMLP_CHUNK
    NC = S // C
    u = jax.nn.gelu(u, approximate=False)
    v = _layernorm(jax.nn.gelu(v, approximate=False), norm_w, norm_b)
    v = v.reshape(B, NC, C, G, cg)
    w = jnp.where(jnp.tril(jnp.ones((C, C), dtype=bool))[None], ws, 0.0).astype(v.dtype)
    f = jnp.einsum('gts,bnsgc->bntgc', w, v) + bs.T[None, None, :, :, None]
    return u * f.reshape(B, S, G * cg)


def _hier_moe(h, wg, bg, we, be, w_gu, w_dn):
    T, D = h.shape
    gl = (h @ wg + bg).astype(jnp.float32)
    g_idx = jnp.argmax(gl, axis=-1)
    p_g = jnp.take_along_axis(jax.nn.softmax(gl, axis=-1), g_idx[:, None], axis=-1)
    el = (h @ we + be).astype(jnp.float32).reshape(T, N_GROUPS, EXPERTS_PER_GROUP)
    el = jnp.take_along_axis(el, g_idx[:, None, None], axis=1)[:, 0]
    top_v, top_i = lax.top_k(el, TOP_K)
    gate = (p_g * jax.nn.softmax(top_v, axis=-1)).reshape(-1)
    eid = (g_idx[:, None] * EXPERTS_PER_GROUP + top_i).reshape(-1).astype(jnp.int32)
    tok = jnp.repeat(jnp.arange(T, dtype=jnp.int32), TOP_K)
    A = T * TOP_K
    order = jnp.argsort(eid)
    e_s, tok_s, gate_s = eid[order], tok[order], gate[order]
    counts = jnp.bincount(eid, length=N_EXPERTS)
    starts = jnp.cumsum(counts) - counts
    padded = (counts + MOE_BLOCK - 1) // MOE_BLOCK * MOE_BLOCK
    pends = jnp.cumsum(padded)
    pstarts = pends - padded
    dest = pstarts[e_s] + jnp.arange(A, dtype=jnp.int32) - starts[e_s]
    NB = -(-A // MOE_BLOCK) + N_EXPERTS
    P = NB * MOE_BLOCK
    slot_tok = jnp.full((P,), T, jnp.int32).at[dest].set(tok_s)
    slot_gate = jnp.zeros((P,), h.dtype).at[dest].set(gate_s.astype(h.dtype))
    blk_exp = jnp.minimum(jnp.searchsorted(pends, jnp.arange(NB, dtype=jnp.int32) * MOE_BLOCK, side='right'),
                          N_EXPERTS - 1)
    h_pad = jnp.concatenate([h, jnp.zeros((1, D), h.dtype)], axis=0)
    xb = h_pad[slot_tok].reshape(NB, MOE_BLOCK, D)

    def expert_block(args):
        xblk, e = args
        a, b = jnp.split(xblk @ w_gu[e], 2, axis=-1)
        return (jax.nn.silu(a) * b) @ w_dn[e]

    yb = lax.map(expert_block, (xb, blk_exp)).reshape(P, D)
    out = jnp.zeros((T + 1, D), h.dtype).at[slot_tok].add(yb * slot_gate[:, None])
    return out[:T]


def setup_inputs(seed: int = 0) -> dict:
    key = jax.random.key(seed)
    ks = jax.random.split(key, 24)
    L, D = DEPTH, D_MODEL

    def nrm(k, shape, s):
        return jax.random.normal(k, shape, jnp.float32) * s

    def gain(k, shape):
        return 1.0 + nrm(k, shape, 0.01)

    return {
        'x': nrm(ks[0], (BATCH, SEQ, D), 1.0),
        'c': nrm(ks[1], (BATCH, D), 1.0),
        'ada_w': nrm(ks[2], (L, D, N_MOD * D), 0.5 * D ** -0.5),
        'ada_b': nrm(ks[3], (L, N_MOD * D), 0.01),
        'mix_norm_w': gain(ks[4], (L, D)),
        'w_in': nrm(ks[5], (L, D, D_IN), D ** -0.5),
        'w_out': nrm(ks[6], (L, D_MIX, D), D_MIX ** -0.5),
        'ret_norm_w': gain(ks[7], (L, RET_WIDTH)),
        'diff_q_norm_w': gain(ks[8], (L, DIFF_QK_DIM)),
        'diff_k_norm_w': gain(ks[9], (L, DIFF_QK_DIM)),
        'diff_lambda': nrm(ks[10], (L, 4, DIFF_QK_DIM), 0.1),
        'diff_subln_w': gain(ks[11], (L, DIFF_V_DIM)),
        'gmlp_norm_w': gain(ks[12], (L, GMLP_WIDTH)),
        'gmlp_norm_b': nrm(ks[13], (L, GMLP_WIDTH), 0.01),
        'gmlp_ws': nrm(ks[14], (L, GMLP_GROUPS, GMLP_CHUNK, GMLP_CHUNK), GMLP_CHUNK ** -0.5),
        'gmlp_bs': gain(ks[15], (L, GMLP_GROUPS, GMLP_CHUNK)),
        'ffn_norm_w': gain(ks[16], (L, D)),
        'router_group_w': nrm(ks[17], (L, D, N_GROUPS), D ** -0.5),
        'router_group_b': nrm(ks[18], (L, N_GROUPS), 0.01),
        'router_expert_w': nrm(ks[19], (L, D, N_EXPERTS), D ** -0.5),
        'router_expert_b': nrm(ks[20], (L, N_EXPERTS), 0.01),
        'expert_w_gate_up': nrm(ks[21], (L, N_EXPERTS, D, 2 * EXPERT_HIDDEN), D ** -0.5),
        'expert_w_down': nrm(ks[22], (L, N_EXPERTS, EXPERT_HIDDEN, D), EXPERT_HIDDEN ** -0.5),
    }


def reference(x, c, ada_w, ada_b, mix_norm_w, w_in, w_out, ret_norm_w, diff_q_norm_w, diff_k_norm_w,
              diff_lambda, diff_subln_w, gmlp_norm_w, gmlp_norm_b, gmlp_ws, gmlp_bs, ffn_norm_w,
              router_group_w, router_group_b, router_expert_w, router_expert_b,
              expert_w_gate_up, expert_w_down):
    B, S, D = x.shape
    c_act = jax.nn.silu(c)
    for l in range(DEPTH):
        mod = c_act @ ada_w[l] + ada_b[l]
        sh1, sc1, g1, sh2, sc2, g2 = jnp.split(mod, N_MOD, axis=-1)
        h = _rmsnorm(x, mix_norm_w[l]) * (1.0 + sc1[:, None]) + sh1[:, None]
        proj = h @ w_in[l]
        rq, rk, rv, rg, dq, dk, dv, gu, gv = jnp.split(proj, IN_SPLITS, axis=-1)
        y_ret = _retention(rq, rk, rv, rg, ret_norm_w[l])
        lam_init = 0.8 - 0.6 * math.exp(-0.3 * l)
        y_diff = _diff_attention(dq, dk, dv, diff_q_norm_w[l], diff_k_norm_w[l], diff_lambda[l],
                                 diff_subln_w[l], lam_init)
        y_gm = _gmlp(gu, gv, gmlp_norm_w[l], gmlp_norm_b[l], gmlp_ws[l], gmlp_bs[l])
        mix = jnp.concatenate([y_ret, y_diff, y_gm], axis=-1) @ w_out[l]
        x = x + g1[:, None] * mix
        h2 = _rmsnorm(x, ffn_norm_w[l]) * (1.0 + sc2[:, None]) + sh2[:, None]
        y = _hier_moe(h2.reshape(B * S, D), router_group_w[l], router_group_b[l], router_expert_w[l],
                      router_expert_b[l], expert_w_gate_up[l], expert_w_down[l]).reshape(B, S, D)
        x = x + g2[:, None] * y
    return x
```

```python
import functools
import math

import jax
import jax.numpy as jnp
import numpy as np
from jax import lax
from jax.experimental import pallas as pl
from jax.experimental.pallas import tpu as pltpu

F32 = jnp.float32
BF16 = jnp.bfloat16

EPS = 1e-6
N_MOD = 6
RET_HEADS = 6
HEAD_DIM = 128
CHUNK = 128
ROPE_BASE = 10000.0
DIFF_HEADS = 6
DIFF_QK_DIM = 64
GMLP_GROUPS = 4
N_GROUPS = 4
EXPERTS_PER_GROUP = 8
N_EXPERTS = N_GROUPS * EXPERTS_PER_GROUP
TOP_K = 2
ROUTER_PAD = 128
NEG_BIG = -1e30
VMEM_LIMIT = 56 * 1024 * 1024


def _pick(n, prefs):
    for p in prefs:
        if n % p == 0:
            return p
    return n


def _params(sem, vmem=VMEM_LIMIT):
    return pltpu.CompilerParams(dimension_semantics=sem, vmem_limit_bytes=vmem)


def _mod_kernel(c_ref, w_ref, b_ref, o_ref):
    ca = jax.nn.silu(c_ref[...])
    o_ref[0] = jnp.dot(ca, w_ref[0], precision=lax.Precision.HIGHEST,
                       preferred_element_type=F32) + b_ref[0]


def _modulation(c, ada_w, ada_b):
    L, D, N = ada_w.shape
    B = c.shape[0]
    tn = _pick(N, (1024, 512, 256, 128))
    return pl.pallas_call(
        _mod_kernel,
        grid=(L, N // tn),
        in_specs=[pl.BlockSpec((B, D), lambda l, j: (0, 0)),
                  pl.BlockSpec((1, D, tn), lambda l, j: (l, 0, j)),
                  pl.BlockSpec((1, 1, tn), lambda l, j: (l, 0, j))],
        out_specs=pl.BlockSpec((1, B, tn), lambda l, j: (l, 0, j)),
        out_shape=jax.ShapeDtypeStruct((L, B, N), F32),
        compiler_params=_params(("parallel", "parallel")),
        name="modulation",
    )(c, ada_w, ada_b.reshape(L, 1, N))


def _inproj_kernel(x_ref, mod_ref, nw_ref, w_ref, o_ref, h_ref):
    @pl.when(pl.program_id(1) == 0)
    def _():
        x = x_ref[...]
        ms = jnp.mean(x * x, axis=-1, keepdims=True)
        y = x * lax.rsqrt(ms + EPS) * nw_ref[...]
        h_ref[...] = (y * (1.0 + mod_ref[0, 1:2, :]) + mod_ref[0, 0:1, :]).astype(BF16)

    o_ref[...] = jnp.dot(h_ref[...], w_ref[...], preferred_element_type=F32).astype(o_ref.dtype)


def _in_projection(x2d, mod, norm_w, w_in_bf16, seq):
    T, D = x2d.shape
    N = w_in_bf16.shape[1]
    tm = _pick(seq, (1024, 512, 256, 128))
    tn = _pick(N, (1280, 640, 128))
    return pl.pallas_call(
        _inproj_kernel,
        grid=(T // tm, N // tn),
        in_specs=[pl.BlockSpec((tm, D), lambda i, j: (i, 0)),
                  pl.BlockSpec((1, N_MOD, D), lambda i, j: ((i * tm) // seq, 0, 0)),
                  pl.BlockSpec((1, D), lambda i, j: (0, 0)),
                  pl.BlockSpec((D, tn), lambda i, j: (0, j))],
        out_specs=pl.BlockSpec((tm, tn), lambda i, j: (i, j)),
        out_shape=jax.ShapeDtypeStruct((T, N), BF16),
        scratch_shapes=[pltpu.VMEM((tm, D), BF16)],
        compiler_params=_params(("parallel", "arbitrary")),
        name="in_projection",
    )(x2d, mod, norm_w.reshape(1, D), w_in_bf16)


def _ret_kernel(q_ref, k_ref, v_ref, g_ref, cos_ref, sin_ref, dec_ref, zeta_ref, xi_ref, nw_ref,
                o_ref, state_ref, *, ts, cdec):
    @pl.when(pl.program_id(1) == 0)
    def _():
        state_ref[...] = jnp.zeros_like(state_ref)

    nt = (((1,), (1,)), ((), ()))
    tn = (((0,), (0,)), ((), ()))
    for c in range(ts // CHUNK):
        rows = slice(c * CHUNK, (c + 1) * CHUNK)
        cosb = cos_ref[rows, :]
        sinb = sin_ref[rows, :]
        for h in range(RET_HEADS):
            cols = slice(h * HEAD_DIM, (h + 1) * HEAD_DIM)
            q = q_ref[0, rows, cols].astype(F32)
            k = k_ref[0, rows, cols].astype(F32)
            v = v_ref[0, rows, cols]
            qr = q * cosb + pltpu.roll(q, HEAD_DIM // 2, 1) * sinb
            kr = (k * cosb + pltpu.roll(k, HEAD_DIM // 2, 1) * sinb) * (HEAD_DIM ** -0.5)
            qb = qr.astype(BF16)
            s = lax.dot_general(qb, kr.astype(BF16), nt, preferred_element_type=F32) * dec_ref[h]
            y = jnp.dot(s.astype(BF16), v, preferred_element_type=F32)
            st = state_ref[h]
            y = y + jnp.dot(qb, st.astype(BF16), preferred_element_type=F32) * xi_ref[h]
            kz = (kr * zeta_ref[h]).astype(BF16)
            kv = lax.dot_general(kz, v, tn, preferred_element_type=F32)
            state_ref[h] = st * cdec[h] + kv
            mu = jnp.mean(y, axis=-1, keepdims=True)
            yc = y - mu
            var = jnp.mean(yc * yc, axis=-1, keepdims=True)
            yn = yc * lax.rsqrt(var + EPS) * nw_ref[:, cols]
            g = g_ref[0, rows, cols].astype(F32)
            o_ref[0, rows, cols] = (jax.nn.silu(g) * yn).astype(o_ref.dtype)


def _retention(proj, norm_w):
    B, S, _ = proj.shape
    W = RET_HEADS * HEAD_DIM
    ts = _pick(S, (512, 256, 128))
    half = HEAD_DIM // 2
    inv = ROPE_BASE ** (-jnp.arange(half, dtype=F32) / half)
    ang = jnp.arange(S).astype(F32)[:, None] * inv[None, :]
    cos_t = jnp.concatenate([jnp.cos(ang), jnp.cos(ang)], axis=-1)
    sin_t = jnp.concatenate([-jnp.sin(ang), jnp.sin(ang)], axis=-1)
    log_g = jnp.log(1.0 - jnp.power(2.0, -5.0 - jnp.arange(RET_HEADS, dtype=F32)))
    idx = jnp.arange(CHUNK, dtype=F32)
    rel = idx[:, None] - idx[None, :]
    decay = jnp.where(rel[None] >= 0, jnp.exp(jnp.maximum(rel, 0.0)[None] * log_g[:, None, None]), 0.0)
    zeta = jnp.exp((CHUNK - 1.0 - idx)[None, :] * log_g[:, None])
    xi = jnp.exp((idx + 1.0)[None, :] * log_g[:, None])
    zeta_b = jnp.broadcast_to(zeta[:, :, None], (RET_HEADS, CHUNK, HEAD_DIM))
    xi_b = jnp.broadcast_to(xi[:, :, None], (RET_HEADS, CHUNK, HEAD_DIM))
    cdec = tuple(float(np.exp(CHUNK * np.log(1.0 - 2.0 ** (-5.0 - h)))) for h in range(RET_HEADS))

    blk = lambda j: pl.BlockSpec((1, ts, W), lambda b, s, j=j: (b, s, j))
    tab = pl.BlockSpec((RET_HEADS, CHUNK, HEAD_DIM), lambda b, s: (0, 0, 0))
    return pl.pallas_call(
        functools.partial(_ret_kernel, ts=ts, cdec=cdec),
        grid=(B, S // ts),
        in_specs=[blk(0), blk(1), blk(2), blk(3),
                  pl.BlockSpec((ts, HEAD_DIM), lambda b, s: (s, 0)),
                  pl.BlockSpec((ts, HEAD_DIM), lambda b, s: (s, 0)),
                  tab, tab, tab,
                  pl.BlockSpec((1, W), lambda b, s: (0, 0))],
        out_specs=pl.BlockSpec((1, ts, W), lambda b, s: (b, s, 0)),
        out_shape=jax.ShapeDtypeStruct((B, S, W), BF16),
        scratch_shapes=[pltpu.VMEM((RET_HEADS, HEAD_DIM, HEAD_DIM), F32)],
        compiler_params=_params(("parallel", "arbitrary")),
        name="retention",
    )(proj, proj, proj, proj, cos_t, sin_t, decay, zeta_b, xi_b, norm_w.reshape(1, W))


def _dprep_kernel(q_ref, k_ref, v_ref, qw_ref, kw_ref, ones_ref, qo_ref, ko_ref, vt_ref, *, tk):
    ones = ones_ref[...]

    def group_rms(x, w):
        sq = x * x
        hi = sq.astype(BF16)
        lo = (sq - hi.astype(F32)).astype(BF16)
        ms = (jnp.dot(hi, ones, preferred_element_type=F32)
              + jnp.dot(lo, ones, preferred_element_type=F32)) * (1.0 / DIFF_QK_DIM)
        return x * lax.rsqrt(ms + EPS) * w

    q = q_ref[0].astype(F32)
    k = k_ref[0].astype(F32)
    qo_ref[0] = (group_rms(q, qw_ref[...]) * (DIFF_QK_DIM ** -0.5)).astype(qo_ref.dtype)
    ko_ref[0] = group_rms(k, kw_ref[...]).astype(ko_ref.dtype)
    ts = q.shape[0]
    for h in range(DIFF_HEADS):
        for t in range(ts // tk):
            vb = v_ref[0, t * tk:(t + 1) * tk, h * HEAD_DIM:(h + 1) * HEAD_DIM].astype(F32)
            vt_ref[0, h, t] = vb.T.astype(vt_ref.dtype)


def _diff_kernel(lam_ref, q_ref, k_ref, vt_ref, sw_ref, o_ref, *, tq, tk, out_scale):
    i = pl.program_id(2)
    lam = lam_ref[0, 0]
    q = q_ref[0]
    lane = lax.broadcasted_iota(jnp.int32, q.shape, 1)
    zero = jnp.zeros_like(q)
    qbd = jnp.concatenate([jnp.where(lane < DIFF_QK_DIM, q, zero),
                           jnp.where(lane >= DIFF_QK_DIM, q, zero)], axis=0)
    nt = (((1,), (1,)), ((), ()))
    n_full = (i * tq) // tk

    def step(j, carry, masked):
        m, l, acc = carry
        kblk = k_ref[0, pl.ds(pl.multiple_of(j * tk, tk), tk), :]
        st = lax.dot_general(kblk, qbd, nt, preferred_element_type=F32)
        if masked:
            kpos = j * tk + lax.broadcasted_iota(jnp.int32, st.shape, 0)
            col = lax.broadcasted_iota(jnp.int32, st.shape, 1)
            qpos = i * tq + jnp.where(col >= tq, col - tq, col)
            st = jnp.where(kpos <= qpos, st, NEG_BIG)
        m_new = jnp.maximum(m, jnp.max(st, axis=0, keepdims=True))
        alpha = jnp.exp(m - m_new)
        p = jnp.exp(st - m_new)
        l = alpha * l + jnp.sum(p, axis=0, keepdims=True)
        acc = alpha * acc + jnp.dot(vt_ref[0, 0, j], p.astype(BF16), preferred_element_type=F32)
        return m_new, l, acc

    init = (jnp.full((1, 2 * tq), NEG_BIG, F32), jnp.zeros((1, 2 * tq), F32),
            jnp.zeros((HEAD_DIM, 2 * tq), F32))
    carry = lax.fori_loop(0, n_full, functools.partial(step, masked=False), init)
    m, l, acc = step(n_full, carry, True)
    o = acc[:, :tq] / l[:, :tq] - lam * (acc[:, tq:] / l[:, tq:])
    ms = jnp.mean(o * o, axis=0, keepdims=True)
    o = o * lax.rsqrt(ms + EPS) * sw_ref[...] * out_scale
    o_ref[0] = o.T.astype(o_ref.dtype)


def _diff_attention(proj, q_norm_w, k_norm_w, lam_params, subln_w, lam_init):
    B, S, _ = proj.shape
    W = DIFF_HEADS * HEAD_DIM
    ts = _pick(S, (512, 256, 128))
    tk = ts
    tq = 128
    grp = jnp.arange(W) // DIFF_QK_DIM
    ones_bd = (grp[:, None] == grp[None, :]).astype(BF16)
    qw = jnp.tile(q_norm_w, W // DIFF_QK_DIM).reshape(1, W)
    kw = jnp.tile(k_norm_w, W // DIFF_QK_DIM).reshape(1, W)
    blk = lambda j: pl.BlockSpec((1, ts, W), lambda b, s, j=j: (b, s, j))
    row = pl.BlockSpec((1, W), lambda b, s: (0, 0))
    qn, kn, vt = pl.pallas_call(
        functools.partial(_dprep_kernel, tk=tk),
        grid=(B, S // ts),
        in_specs=[blk(4), blk(5), blk(6), row, row, pl.BlockSpec((W, W), lambda b, s: (0, 0))],
        out_specs=[pl.BlockSpec((1, ts, W), lambda b, s: (b, s, 0)),
                   pl.BlockSpec((1, ts, W), lambda b, s: (b, s, 0)),
                   pl.BlockSpec((1, DIFF_HEADS, ts // tk, HEAD_DIM, tk), lambda b, s: (b, 0, s, 0, 0))],
        out_shape=[jax.ShapeDtypeStruct((B, S, W), BF16),
                   jax.ShapeDtypeStruct((B, S, W), BF16),
                   jax.ShapeDtypeStruct((B, DIFF_HEADS, S // tk, HEAD_DIM, tk), BF16)],
        compiler_params=_params(("parallel", "parallel")),
        name="diff_prep",
    )(proj, proj, proj, qw, kw, ones_bd)

    lp = lam_params.astype(F32)
    lam = jnp.exp(jnp.sum(lp[0] * lp[1])) - jnp.exp(jnp.sum(lp[2] * lp[3])) + lam_init
    return pl.pallas_call(
        functools.partial(_diff_kernel, tq=tq, tk=tk, out_scale=1.0 - lam_init),
        grid=(B, DIFF_HEADS, S // tq),
        in_specs=[pl.BlockSpec(memory_space=pltpu.SMEM),
                  pl.BlockSpec((1, tq, HEAD_DIM), lambda b, h, i: (b, i, h)),
                  pl.BlockSpec((1, S, HEAD_DIM), lambda b, h, i: (b, 0, h)),
                  pl.BlockSpec((1, 1, S // tk, HEAD_DIM, tk), lambda b, h, i: (b, h, 0, 0, 0)),
                  pl.BlockSpec((HEAD_DIM, 1), lambda b, h, i: (0, 0))],
        out_specs=pl.BlockSpec((1, tq, HEAD_DIM), lambda b, h, i: (b, i, h)),
        out_shape=jax.ShapeDtypeStruct((B, S, W), BF16),
        compiler_params=_params(("parallel", "parallel", "parallel")),
        name="diff_attention",
    )(lam.reshape(1, 1), qn, kn, vt, subln_w.reshape(HEAD_DIM, 1))


def _gelu_exact(x):
    return 0.5 * x * (1.0 + lax.erf(x * (2.0 ** -0.5)))


def _gmlp_kernel(ua_ref, ub_ref, va_ref, vb_ref, nw_ref, nb_ref, ws_ref, bs_ref, o_ref, *, ts):
    u = jnp.concatenate([ua_ref[0], ub_ref[0]], axis=-1).astype(F32)
    v = jnp.concatenate([va_ref[0], vb_ref[0]], axis=-1).astype(F32)
    u = _gelu_exact(u)
    v = _gelu_exact(v)
    mu = jnp.mean(v, axis=-1, keepdims=True)
    vc = v - mu
    var = jnp.mean(vc * vc, axis=-1, keepdims=True)
    v = (vc * lax.rsqrt(var + EPS) * nw_ref[...] + nb_ref[...]).astype(BF16)
    r = lax.broadcasted_iota(jnp.int32, (CHUNK, CHUNK), 0)
    cc = lax.broadcasted_iota(jnp.int32, (CHUNK, CHUNK), 1)
    for g in range(GMLP_GROUPS):
        cols = slice(g * HEAD_DIM, (g + 1) * HEAD_DIM)
        w = jnp.where(r >= cc, ws_ref[g], 0.0).astype(BF16)
        for c in range(ts // CHUNK):
            rows = slice(c * CHUNK, (c + 1) * CHUNK)
            f = jnp.dot(w, v[rows, cols], preferred_element_type=F32) + bs_ref[g]
            o_ref[0, rows, cols] = (u[rows, cols] * f).astype(o_ref.dtype)


def _gmlp(proj, norm_w, norm_b, ws, bs):
    B, S, N = proj.shape
    W = GMLP_GROUPS * HEAD_DIM
    ts = _pick(S, (512, 256, 128))
    half = W // 2
    base = (N - 2 * W) // half
    blk = lambda j: pl.BlockSpec((1, ts, half), lambda b, s, j=j: (b, s, base + j))
    row = pl.BlockSpec((1, W), lambda b, s: (0, 0))
    tab = pl.BlockSpec((GMLP_GROUPS, CHUNK, CHUNK), lambda b, s: (0, 0, 0))
    bs_b = jnp.broadcast_to(bs[:, :, None], (GMLP_GROUPS, CHUNK, HEAD_DIM))
    return pl.pallas_call(
        functools.partial(_gmlp_kernel, ts=ts),
        grid=(B, S // ts),
        in_specs=[blk(0), blk(1), blk(2), blk(3), row, row, tab, tab],
        out_specs=pl.BlockSpec((1, ts, W), lambda b, s: (b, s, 0)),
        out_shape=jax.ShapeDtypeStruct((B, S, W), BF16),
        compiler_params=_params(("parallel", "parallel")),
        name="gmlp",
    )(proj, proj, proj, proj, norm_w.reshape(1, W), norm_b.reshape(1, W), ws, bs_b)


def _outproj_kernel(yr_ref, yd_ref, yg_ref, w_ref, x_ref, mod_ref, nw_ref, rhi_ref, rlo_ref, rb_ref,
                    x1_ref, h2_ref, lg_ref):
    cat = jnp.concatenate([yr_ref[...], yd_ref[...], yg_ref[...]], axis=-1)
    mix = jnp.dot(cat, w_ref[...], preferred_element_type=F32)
    x1 = x_ref[...] + mod_ref[0, 2:3, :] * mix
    x1_ref[...] = x1
    ms = jnp.mean(x1 * x1, axis=-1, keepdims=True)
    h2 = x1 * lax.rsqrt(ms + EPS) * nw_ref[...]
    h2 = h2 * (1.0 + mod_ref[0, 4:5, :]) + mod_ref[0, 3:4, :]
    hi = h2.astype(BF16)
    lo = (h2 - hi.astype(F32)).astype(BF16)
    h2_ref[...] = hi
    rhi = rhi_ref[...]
    lg_ref[...] = (jnp.dot(hi, rhi, preferred_element_type=F32)
                   + jnp.dot(lo, rhi, preferred_element_type=F32)
                   + jnp.dot(hi, rlo_ref[...], preferred_element_type=F32)) + rb_ref[...]


def _out_projection(y_ret, y_diff, y_gm, w_out_bf16, x2d, mod, norm_w, r_hi, r_lo, r_b, seq):
    T, D = x2d.shape
    tm = _pick(seq, (512, 256, 128))
    rowblk = lambda a: pl.BlockSpec((tm, a.shape[1]), lambda i: (i, 0))
    full = lambda a: pl.BlockSpec(a.shape, lambda i: (0, 0))
    nw = norm_w.reshape(1, D)
    return pl.pallas_call(
        _outproj_kernel,
        grid=(T // tm,),
        in_specs=[rowblk(y_ret), rowblk(y_diff), rowblk(y_gm), full(w_out_bf16), rowblk(x2d),
                  pl.BlockSpec((1, N_MOD, D), lambda i: ((i * tm) // seq, 0, 0)),
                  full(nw), full(r_hi), full(r_lo), full(r_b)],
        out_specs=[pl.BlockSpec((tm, D), lambda i: (i, 0)),
                   pl.BlockSpec((tm, D), lambda i: (i, 0)),
                   pl.BlockSpec((tm, ROUTER_PAD), lambda i: (i, 0))],
        out_shape=[jax.ShapeDtypeStruct((T, D), F32),
                   jax.ShapeDtypeStruct((T, D), BF16),
                   jax.ShapeDtypeStruct((T, ROUTER_PAD), F32)],
        compiler_params=_params(("parallel",)),
        name="out_projection",
    )(y_ret, y_diff, y_gm, w_out_bf16, x2d, mod, nw, r_hi, r_lo, r_b)


def _moe_kernel(te_ref, ti_ref, nv_ref, x_ref, wgu_ref, wdn_ref, o_ref, *, hidden):
    @pl.when(pl.program_id(0) < nv_ref[0])
    def _():
        ab = jnp.dot(x_ref[...], wgu_ref[0], preferred_element_type=F32)
        hmid = (jax.nn.silu(ab[:, :hidden]) * ab[:, hidden:]).astype(BF16)
        o_ref[...] = jnp.dot(hmid, wdn_ref[0], preferred_element_type=F32).astype(o_ref.dtype)


def _expert_ffn(xb, tile_exp, tile_idx, n_valid, w_gu_bf16, w_dn_bf16, tile_m):
    P, D = xb.shape
    hidden = w_dn_bf16.shape[1]
    grid_spec = pltpu.PrefetchScalarGridSpec(
        num_scalar_prefetch=3,
        grid=(P // tile_m,),
        in_specs=[pl.BlockSpec((tile_m, D), lambda i, te, ti, nv: (ti[i], 0)),
                  pl.BlockSpec((1, D, 2 * hidden), lambda i, te, ti, nv: (te[i], 0, 0)),
                  pl.BlockSpec((1, hidden, D), lambda i, te, ti, nv: (te[i], 0, 0))],
        out_specs=pl.BlockSpec((tile_m, D), lambda i, te, ti, nv: (ti[i], 0)),
    )
    return pl.pallas_call(
        functools.partial(_moe_kernel, hidden=hidden),
        grid_spec=grid_spec,
        out_shape=jax.ShapeDtypeStruct((P, D), F32),
        compiler_params=_params(("arbitrary",)),
        name="expert_ffn",
    )(tile_exp, tile_idx, n_valid, xb, w_gu_bf16, w_dn_bf16)


def _route(logits, tile_m):
    T = logits.shape[0]
    gl = logits[:, :N_GROUPS]
    el = logits[:, N_GROUPS:N_GROUPS + N_EXPERTS].reshape(T, N_GROUPS, EXPERTS_PER_GROUP)
    g_idx = jnp.argmax(gl, axis=-1)
    p_g = jnp.take_along_axis(jax.nn.softmax(gl, axis=-1), g_idx[:, None], axis=-1)
    el = jnp.take_along_axis(el, g_idx[:, None, None], axis=1)[:, 0]
    top_v, top_i = lax.top_k(el, TOP_K)
    gate = p_g * jax.nn.softmax(top_v, axis=-1)
    eid = (g_idx[:, None] * EXPERTS_PER_GROUP + top_i).reshape(-1).astype(jnp.int32)
    A = T * TOP_K
    order = jnp.argsort(eid).astype(jnp.int32)
    e_s = eid[order]
    counts = jnp.sum((eid[:, None] == jnp.arange(N_EXPERTS, dtype=jnp.int32)[None, :]).astype(jnp.int32), axis=0)
    starts = jnp.cumsum(counts) - counts
    ptiles = (counts + tile_m - 1) // tile_m
    pends = jnp.cumsum(ptiles)
    pstarts = pends - ptiles
    n_tiles = -(-A // tile_m) + N_EXPERTS
    n_valid = pends[-1].astype(jnp.int32)
    tiles = jnp.arange(n_tiles, dtype=jnp.int32)
    tile_idx = jnp.minimum(tiles, n_valid - 1)
    tile_exp = jnp.minimum(jnp.searchsorted(pends, tile_idx, side='right'), N_EXPERTS - 1).astype(jnp.int32)
    slot = jnp.arange(n_tiles * tile_m, dtype=jnp.int32)
    s_exp = tile_exp[slot // tile_m]
    s_rank = slot - pstarts[s_exp] * tile_m
    s_ok = (slot // tile_m < n_valid) & (s_rank < counts[s_exp])
    s_src = jnp.clip(starts[s_exp] + s_rank, 0, A - 1)
    slot_tok = jnp.where(s_ok, order[s_src] // TOP_K, 0)
    dest_s = pstarts[e_s] * tile_m + jnp.arange(A, dtype=jnp.int32) - starts[e_s]
    dest = jnp.zeros((A,), jnp.int32).at[order].set(dest_s).reshape(T, TOP_K)
    return gate, dest, slot_tok, tile_exp, tile_idx, n_valid.reshape(1)


def kernel(x, c, ada_w, ada_b, mix_norm_w, w_in, w_out, ret_norm_w, diff_q_norm_w, diff_k_norm_w, diff_lambda, diff_subln_w, gmlp_norm_w, gmlp_norm_b, gmlp_ws, gmlp_bs, ffn_norm_w, router_group_w, router_group_b, router_expert_w, router_expert_b, expert_w_gate_up, expert_w_down):
    B, S, D = x.shape
    L = ada_w.shape[0]
    T = B * S
    tile_m = 256
    mod_all = _modulation(c, ada_w, ada_b).reshape(L, B, N_MOD, D)
    x2d = x.reshape(T, D)
    for l in range(L):
        mod = mod_all[l]
        proj = _in_projection(x2d, mod, mix_norm_w[l], w_in[l].astype(BF16), S).reshape(B, S, -1)
        lam_init = 0.8 - 0.6 * math.exp(-0.3 * l)
        y_ret = _retention(proj, ret_norm_w[l])
        y_diff = _diff_attention(proj, diff_q_norm_w[l], diff_k_norm_w[l], diff_lambda[l],
                                 diff_subln_w[l], lam_init)
        y_gm = _gmlp(proj, gmlp_norm_w[l], gmlp_norm_b[l], gmlp_ws[l], gmlp_bs[l])

        r_w = jnp.concatenate([router_group_w[l], router_expert_w[l]], axis=1)
        r_w = jnp.pad(r_w, ((0, 0), (0, ROUTER_PAD - r_w.shape[1])))
        r_hi = r_w.astype(BF16)
        r_lo = (r_w - r_hi.astype(F32)).astype(BF16)
        r_b = jnp.pad(jnp.concatenate([router_group_b[l], router_expert_b[l]]),
                      (0, ROUTER_PAD - N_GROUPS - N_EXPERTS)).reshape(1, ROUTER_PAD)
        x1, h2, logits = _out_projection(
            y_ret.reshape(T, -1), y_diff.reshape(T, -1), y_gm.reshape(T, -1), w_out[l].astype(BF16),
            x2d, mod, ffn_norm_w[l], r_hi, r_lo, r_b, S)

        gate, dest, slot_tok, tile_exp, tile_idx, n_valid = _route(logits, tile_m)
        xb = h2[slot_tok]
        yb = _expert_ffn(xb, tile_exp, tile_idx, n_valid, expert_w_gate_up[l].astype(BF16),
                         expert_w_down[l].astype(BF16), tile_m)
        y = yb[dest[:, 0]] * gate[:, 0:1] + yb[dest[:, 1]] * gate[:, 1:2]
        g2 = jnp.broadcast_to(mod[:, None, 5, :], (B, S, D)).reshape(T, D)
        x2d = x1 + g2 * y
    return x2d.reshape(B, S, D)
```

```python
import functools
import math

import jax
import jax.numpy as jnp
import numpy as np
from jax import lax
from jax.experimental import pallas as pl
from jax.experimental.pallas import tpu as pltpu

F32 = jnp.float32
BF16 = jnp.bfloat16

EPS = 1e-6
N_MOD = 6
RET_HEADS = 6
HEAD_DIM = 128
CHUNK = 128
ROPE_BASE = 10000.0
DIFF_HEADS = 6
DIFF_QK_DIM = 64
GMLP_GROUPS = 4
N_GROUPS = 4
EXPERTS_PER_GROUP = 8
N_EXPERTS = N_GROUPS * EXPERTS_PER_GROUP
TOP_K = 2
ROUTER_PAD = 128
NEG_BIG = -1e30
LOG2E = math.log2(math.e)
VMEM_LIMIT = 56 * 1024 * 1024


def _pick(n, prefs):
    for p in prefs:
        if n % p == 0:
            return p
    return n


def _params(sem, vmem=VMEM_LIMIT):
    return pltpu.CompilerParams(dimension_semantics=sem, vmem_limit_bytes=vmem)


def _mod_kernel(c_ref, w_ref, b_ref, o_ref):
    ca = jax.nn.silu(c_ref[...])
    o_ref[0] = jnp.dot(ca, w_ref[0], precision=lax.Precision.HIGHEST,
                       preferred_element_type=F32) + b_ref[0]


def _modulation(c, ada_w, ada_b):
    L, D, N = ada_w.shape
    B = c.shape[0]
    tn = _pick(N, (1024, 512, 256, 128))
    return pl.pallas_call(
        _mod_kernel,
        grid=(L, N // tn),
        in_specs=[pl.BlockSpec((B, D), lambda l, j: (0, 0)),
                  pl.BlockSpec((1, D, tn), lambda l, j: (l, 0, j)),
                  pl.BlockSpec((1, 1, tn), lambda l, j: (l, 0, j))],
        out_specs=pl.BlockSpec((1, B, tn), lambda l, j: (l, 0, j)),
        out_shape=jax.ShapeDtypeStruct((L, B, N), F32),
        compiler_params=_params(("parallel", "parallel")),
        name="modulation",
    )(c, ada_w, ada_b.reshape(L, 1, N))


def _inproj_kernel(x_ref, mod_ref, nw_ref, w_ref, o_ref, h_ref):
    @pl.when(pl.program_id(1) == 0)
    def _():
        x = x_ref[...]
        ms = jnp.mean(x * x, axis=-1, keepdims=True)
        y = x * lax.rsqrt(ms + EPS) * nw_ref[...]
        h_ref[...] = (y * (1.0 + mod_ref[0, 1:2, :]) + mod_ref[0, 0:1, :]).astype(BF16)

    o_ref[...] = jnp.dot(h_ref[...], w_ref[...], preferred_element_type=F32).astype(o_ref.dtype)


def _in_projection(x2d, mod, norm_w, w_in_bf16, seq):
    T, D = x2d.shape
    N = w_in_bf16.shape[1]
    tm = _pick(seq, (1024, 512, 256, 128))
    tn = _pick(N, (1280, 640, 128))
    return pl.pallas_call(
        _inproj_kernel,
        grid=(T // tm, N // tn),
        in_specs=[pl.BlockSpec((tm, D), lambda i, j: (i, 0)),
                  pl.BlockSpec((1, N_MOD, D), lambda i, j: ((i * tm) // seq, 0, 0)),
                  pl.BlockSpec((1, D), lambda i, j: (0, 0)),
                  pl.BlockSpec((D, tn), lambda i, j: (0, j))],
        out_specs=pl.BlockSpec((tm, tn), lambda i, j: (i, j)),
        out_shape=jax.ShapeDtypeStruct((T, N), BF16),
        scratch_shapes=[pltpu.VMEM((tm, D), BF16)],
        compiler_params=_params(("parallel", "arbitrary")),
        name="in_projection",
    )(x2d, mod, norm_w.reshape(1, D), w_in_bf16)


def _ret_kernel(q_ref, k_ref, v_ref, g_ref, cos_ref, sin_ref, dec_ref, zeta_ref, xi_ref, nw_ref,
                o_ref, state_ref, *, ts, cdec):
    @pl.when(pl.program_id(1) == 0)
    def _():
        state_ref[...] = jnp.zeros_like(state_ref)

    nt = (((1,), (1,)), ((), ()))
    tn = (((0,), (0,)), ((), ()))
    for c in range(ts // CHUNK):
        rows = slice(c * CHUNK, (c + 1) * CHUNK)
        cosb = cos_ref[rows, :]
        sinb = sin_ref[rows, :]
        for h in range(RET_HEADS):
            cols = slice(h * HEAD_DIM, (h + 1) * HEAD_DIM)
            q = q_ref[0, rows, cols].astype(F32)
            k = k_ref[0, rows, cols].astype(F32)
            v = v_ref[0, rows, cols]
            qr = q * cosb + pltpu.roll(q, HEAD_DIM // 2, 1) * sinb
            kr = (k * cosb + pltpu.roll(k, HEAD_DIM // 2, 1) * sinb) * (HEAD_DIM ** -0.5)
            qb = qr.astype(BF16)
            s = lax.dot_general(qb, kr.astype(BF16), nt, preferred_element_type=F32) * dec_ref[h]
            y = jnp.dot(s.astype(BF16), v, preferred_element_type=F32)
            st = state_ref[h]
            y = y + jnp.dot(qb, st.astype(BF16), preferred_element_type=F32) * xi_ref[h]
            kz = (kr * zeta_ref[h]).astype(BF16)
            kv = lax.dot_general(kz, v, tn, preferred_element_type=F32)
            state_ref[h] = st * cdec[h] + kv
            mu = jnp.mean(y, axis=-1, keepdims=True)
            yc = y - mu
            var = jnp.mean(yc * yc, axis=-1, keepdims=True)
            yn = yc * lax.rsqrt(var + EPS) * nw_ref[:, cols]
            g = g_ref[0, rows, cols].astype(F32)
            o_ref[0, rows, cols] = (jax.nn.silu(g) * yn).astype(o_ref.dtype)


def _retention(proj, norm_w):
    B, S, _ = proj.shape
    W = RET_HEADS * HEAD_DIM
    ts = _pick(S, (512, 256, 128))
    half = HEAD_DIM // 2
    inv = ROPE_BASE ** (-jnp.arange(half, dtype=F32) / half)
    ang = jnp.arange(S).astype(F32)[:, None] * inv[None, :]
    cos_t = jnp.concatenate([jnp.cos(ang), jnp.cos(ang)], axis=-1)
    sin_t = jnp.concatenate([-jnp.sin(ang), jnp.sin(ang)], axis=-1)
    log_g = jnp.log(1.0 - jnp.power(2.0, -5.0 - jnp.arange(RET_HEADS, dtype=F32)))
    idx = jnp.arange(CHUNK, dtype=F32)
    rel = idx[:, None] - idx[None, :]
    decay = jnp.where(rel[None] >= 0, jnp.exp(jnp.maximum(rel, 0.0)[None] * log_g[:, None, None]), 0.0)
    zeta = jnp.exp((CHUNK - 1.0 - idx)[None, :] * log_g[:, None])
    xi = jnp.exp((idx + 1.0)[None, :] * log_g[:, None])
    zeta_b = jnp.broadcast_to(zeta[:, :, None], (RET_HEADS, CHUNK, HEAD_DIM))
    xi_b = jnp.broadcast_to(xi[:, :, None], (RET_HEADS, CHUNK, HEAD_DIM))
    cdec = tuple(float(np.exp(CHUNK * np.log(1.0 - 2.0 ** (-5.0 - h)))) for h in range(RET_HEADS))

    blk = lambda j: pl.BlockSpec((1, ts, W), lambda b, s, j=j: (b, s, j))
    tab = pl.BlockSpec((RET_HEADS, CHUNK, HEAD_DIM), lambda b, s: (0, 0, 0))
    return pl.pallas_call(
        functools.partial(_ret_kernel, ts=ts, cdec=cdec),
        grid=(B, S // ts),
        in_specs=[blk(0), blk(1), blk(2), blk(3),
                  pl.BlockSpec((ts, HEAD_DIM), lambda b, s: (s, 0)),
                  pl.BlockSpec((ts, HEAD_DIM), lambda b, s: (s, 0)),
                  tab, tab, tab,
                  pl.BlockSpec((1, W), lambda b, s: (0, 0))],
        out_specs=pl.BlockSpec((1, ts, W), lambda b, s: (b, s, 0)),
        out_shape=jax.ShapeDtypeStruct((B, S, W), BF16),
        scratch_shapes=[pltpu.VMEM((RET_HEADS, HEAD_DIM, HEAD_DIM), F32)],
        compiler_params=_params(("parallel", "arbitrary")),
        name="retention",
    )(proj, proj, proj, proj, cos_t, sin_t, decay, zeta_b, xi_b, norm_w.reshape(1, W))


def _dprep_kernel(q_ref, k_ref, v_ref, qw_ref, kw_ref, ones_ref, qo_ref, ko_ref, vt_ref, *, tk):
    ones = ones_ref[...]

    def group_rms(x, w):
        sq = x * x
        hi = sq.astype(BF16)
        lo = (sq - hi.astype(F32)).astype(BF16)
        ms = (jnp.dot(hi, ones, preferred_element_type=F32)
              + jnp.dot(lo, ones, preferred_element_type=F32)) * (1.0 / DIFF_QK_DIM)
        return x * lax.rsqrt(ms + EPS) * w

    q = q_ref[0].astype(F32)
    k = k_ref[0].astype(F32)
    qo_ref[0] = (group_rms(q, qw_ref[...]) * (DIFF_QK_DIM ** -0.5 * LOG2E)).astype(qo_ref.dtype)
    ko_ref[0] = group_rms(k, kw_ref[...]).astype(ko_ref.dtype)
    ts = q.shape[0]
    for h in range(DIFF_HEADS):
        for t in range(ts // tk):
            vb = v_ref[0, t * tk:(t + 1) * tk, h * HEAD_DIM:(h + 1) * HEAD_DIM].astype(F32)
            vt_ref[0, h, t] = vb.T.astype(vt_ref.dtype)


def _diff_kernel(lam_ref, q_ref, k_ref, vt_ref, sw_ref, o_ref, qbd_ref, m_ref, l_ref, acc_ref,
                 *, tq, tk, out_scale):
    i = pl.program_id(1)
    lam = lam_ref[0, 0]
    nt = (((1,), (1,)), ((), ()))
    lane = lax.broadcasted_iota(jnp.int32, (tq, HEAD_DIM), 1)
    for h in range(DIFF_HEADS):
        q = q_ref[0, :, h * HEAD_DIM:(h + 1) * HEAD_DIM]
        zero = jnp.zeros_like(q)
        qbd_ref[h, :tq, :] = jnp.where(lane < DIFF_QK_DIM, q, zero)
        qbd_ref[h, tq:, :] = jnp.where(lane >= DIFF_QK_DIM, q, zero)
    m_ref[...] = jnp.full(m_ref.shape, NEG_BIG, F32)
    l_ref[...] = jnp.zeros(l_ref.shape, F32)
    acc_ref[...] = jnp.zeros(acc_ref.shape, F32)
    n_full = (i * tq) // tk

    def step(j, masked):
        row0 = pl.multiple_of(j * tk, tk)
        for h in range(DIFF_HEADS):
            kblk = k_ref[0, pl.ds(row0, tk), h * HEAD_DIM:(h + 1) * HEAD_DIM]
            st = lax.dot_general(kblk, qbd_ref[h], nt, preferred_element_type=F32)
            if masked:
                kpos = j * tk + lax.broadcasted_iota(jnp.int32, st.shape, 0)
                col = lax.broadcasted_iota(jnp.int32, st.shape, 1)
                qpos = i * tq + jnp.where(col >= tq, col - tq, col)
                st = jnp.where(kpos <= qpos, st, NEG_BIG)
            m = m_ref[h]
            m_new = jnp.maximum(m, jnp.max(st, axis=0, keepdims=True))
            alpha = jnp.exp2(m - m_new)
            p = jnp.exp2(st - m_new)
            m_ref[h] = m_new
            l_ref[h] = alpha * l_ref[h] + jnp.sum(p, axis=0, keepdims=True)
            acc_ref[h] = alpha * acc_ref[h] + jnp.dot(vt_ref[0, h, j], p.astype(BF16),
                                                      preferred_element_type=F32)

    def body(j, carry):
        step(j, False)
        return carry

    lax.fori_loop(0, n_full, body, 0)
    step(n_full, True)
    for h in range(DIFF_HEADS):
        acc = acc_ref[h]
        l = l_ref[h]
        o = acc[:, :tq] / l[:, :tq] - lam * (acc[:, tq:] / l[:, tq:])
        ms = jnp.mean(o * o, axis=0, keepdims=True)
        o = o * lax.rsqrt(ms + EPS) * sw_ref[...] * out_scale
        o_ref[0, :, h * HEAD_DIM:(h + 1) * HEAD_DIM] = o.T.astype(o_ref.dtype)


def _diff_attention(proj, q_norm_w, k_norm_w, lam_params, subln_w, lam_init):
    B, S, _ = proj.shape
    W = DIFF_HEADS * HEAD_DIM
    ts = _pick(S, (512, 256, 128))
    tk = ts
    tq = _pick(S, (256, 128))
    grp =jnp.arange(W) // DIFF_QK_DIM
    ones_bd = (grp[:, None] == grp[None, :]).astype(BF16)
    qw = jnp.tile(q_norm_w, W // DIFF_QK_DIM).reshape(1, W)
    kw = jnp.tile(k_norm_w, W // DIFF_QK_DIM).reshape(1, W)
    blk = lambda j: pl.BlockSpec((1, ts, W), lambda b, s, j=j: (b, s, j))
    row = pl.BlockSpec((1, W), lambda b, s: (0, 0))
    qn, kn, vt = pl.pallas_call(
        functools.partial(_dprep_kernel, tk=tk),
        grid=(B, S // ts),
        in_specs=[blk(4), blk(5), blk(6), row, row, pl.BlockSpec((W, W), lambda b, s: (0, 0))],
        out_specs=[pl.BlockSpec((1, ts, W), lambda b, s: (b, s, 0)),
                   pl.BlockSpec((1, ts, W), lambda b, s: (b, s, 0)),
                   pl.BlockSpec((1, DIFF_HEADS, ts // tk, HEAD_DIM, tk), lambda b, s: (b, 0, s, 0, 0))],
        out_shape=[jax.ShapeDtypeStruct((B, S, W), BF16),
                   jax.ShapeDtypeStruct((B, S, W), BF16),
                   jax.ShapeDtypeStruct((B, DIFF_HEADS, S // tk, HEAD_DIM, tk), BF16)],
        compiler_params=_params(("parallel", "parallel")),
        name="diff_prep",
    )(proj, proj, proj, qw, kw, ones_bd)

    lp = lam_params.astype(F32)
    lam = jnp.exp(jnp.sum(lp[0] * lp[1])) - jnp.exp(jnp.sum(lp[2] * lp[3])) + lam_init
    return pl.pallas_call(
        functools.partial(_diff_kernel, tq=tq, tk=tk, out_scale=1.0 - lam_init),
        grid=(B, S // tq),
        in_specs=[pl.BlockSpec(memory_space=pltpu.SMEM),
                  pl.BlockSpec((1, tq, W), lambda b, i: (b, i, 0)),
                  pl.BlockSpec((1, S, W), lambda b, i: (b, 0, 0)),
                  pl.BlockSpec((1, DIFF_HEADS, S // tk, HEAD_DIM, tk), lambda b, i: (b, 0, 0, 0, 0)),
                  pl.BlockSpec((HEAD_DIM, 1), lambda b, i: (0, 0))],
        out_specs=pl.BlockSpec((1, tq, W), lambda b, i: (b, i, 0)),
        out_shape=jax.ShapeDtypeStruct((B, S, W), BF16),
        scratch_shapes=[pltpu.VMEM((DIFF_HEADS, 2 * tq, HEAD_DIM), BF16),
                        pltpu.VMEM((DIFF_HEADS, 1, 2 * tq), F32),
                        pltpu.VMEM((DIFF_HEADS, 1, 2 * tq), F32),
                        pltpu.VMEM((DIFF_HEADS, HEAD_DIM, 2 * tq), F32)],
        compiler_params=_params(("parallel", "parallel")),
        name="diff_attention",
    )(lam.reshape(1, 1), qn, kn, vt, subln_w.reshape(HEAD_DIM, 1))


def _gelu_exact(x):
    return 0.5 * x * (1.0 + lax.erf(x * (2.0 ** -0.5)))


def _gmlp_kernel(ua_ref, ub_ref, va_ref, vb_ref, nw_ref, nb_ref, ws_ref, bs_ref, o_ref, *, ts):
    u = jnp.concatenate([ua_ref[0], ub_ref[0]], axis=-1).astype(F32)
    v = jnp.concatenate([va_ref[0], vb_ref[0]], axis=-1).astype(F32)
    u = _gelu_exact(u)
    v = _gelu_exact(v)
    mu = jnp.mean(v, axis=-1, keepdims=True)
    vc = v - mu
    var = jnp.mean(vc * vc, axis=-1, keepdims=True)
    v = (vc * lax.rsqrt(var + EPS) * nw_ref[...] + nb_ref[...]).astype(BF16)
    r = lax.broadcasted_iota(jnp.int32, (CHUNK, CHUNK), 0)
    cc = lax.broadcasted_iota(jnp.int32, (CHUNK, CHUNK), 1)
    for g in range(GMLP_GROUPS):
        cols = slice(g * HEAD_DIM, (g + 1) * HEAD_DIM)
        w = jnp.where(r >= cc, ws_ref[g], 0.0).astype(BF16)
        for c in range(ts // CHUNK):
            rows = slice(c * CHUNK, (c + 1) * CHUNK)
            f = jnp.dot(w, v[rows, cols], preferred_element_type=F32) + bs_ref[g]
            o_ref[0, rows, cols] = (u[rows, cols] * f).astype(o_ref.dtype)


def _gmlp(proj, norm_w, norm_b, ws, bs):
    B, S, N = proj.shape
    W = GMLP_GROUPS * HEAD_DIM
    ts = _pick(S, (512, 256, 128))
    half = W // 2
    base = (N - 2 * W) // half
    blk = lambda j: pl.BlockSpec((1, ts, half), lambda b, s, j=j: (b, s, base + j))
    row = pl.BlockSpec((1, W), lambda b, s: (0, 0))
    tab = pl.BlockSpec((GMLP_GROUPS, CHUNK, CHUNK), lambda b, s: (0, 0, 0))
    bs_b = jnp.broadcast_to(bs[:, :, None], (GMLP_GROUPS, CHUNK, HEAD_DIM))
    return pl.pallas_call(
        functools.partial(_gmlp_kernel, ts=ts),
        grid=(B, S // ts),
        in_specs=[blk(0), blk(1), blk(2), blk(3), row, row, tab, tab],
        out_specs=pl.BlockSpec((1, ts, W), lambda b, s: (b, s, 0)),
        out_shape=jax.ShapeDtypeStruct((B, S, W), BF16),
        compiler_params=_params(("parallel", "parallel")),
        name="gmlp",
    )(proj, proj, proj, proj, norm_w.reshape(1, W), norm_b.reshape(1, W), ws, bs_b)


def _outproj_kernel(yr_ref, yd_ref, yg_ref, w_ref, x_ref, mod_ref, nw_ref, rhi_ref, rlo_ref, rb_ref,
                    x1_ref, h2_ref, lg_ref):
    cat = jnp.concatenate([yr_ref[...], yd_ref[...], yg_ref[...]], axis=-1)
    mix = jnp.dot(cat, w_ref[...], preferred_element_type=F32)
    x1 = x_ref[...] + mod_ref[0, 2:3, :] * mix
    x1_ref[...] = x1
    ms = jnp.mean(x1 * x1, axis=-1, keepdims=True)
    h2 = x1 * lax.rsqrt(ms + EPS) * nw_ref[...]
    h2 = h2 * (1.0 + mod_ref[0, 4:5, :]) + mod_ref[0, 3:4, :]
    hi = h2.astype(BF16)
    lo = (h2 - hi.astype(F32)).astype(BF16)
    h2_ref[...] = hi
    rhi = rhi_ref[...]
    lg_ref[...] = (jnp.dot(hi, rhi, preferred_element_type=F32)
                   + jnp.dot(lo, rhi, preferred_element_type=F32)
                   + jnp.dot(hi, rlo_ref[...], preferred_element_type=F32)) + rb_ref[...]


def _out_projection(y_ret, y_diff, y_gm, w_out_bf16, x2d, mod, norm_w, r_hi, r_lo, r_b, seq):
    T, D = x2d.shape
    tm = _pick(seq, (512, 256, 128))
    rowblk = lambda a: pl.BlockSpec((tm, a.shape[1]), lambda i: (i, 0))
    full = lambda a: pl.BlockSpec(a.shape, lambda i: (0, 0))
    nw = norm_w.reshape(1, D)
    return pl.pallas_call(
        _outproj_kernel,
        grid=(T // tm,),
        in_specs=[rowblk(y_ret), rowblk(y_diff), rowblk(y_gm), full(w_out_bf16), rowblk(x2d),
                  pl.BlockSpec((1, N_MOD, D), lambda i: ((i * tm) // seq, 0, 0)),
                  full(nw), full(r_hi), full(r_lo), full(r_b)],
        out_specs=[pl.BlockSpec((tm, D), lambda i: (i, 0)),
                   pl.BlockSpec((tm, D), lambda i: (i, 0)),
                   pl.BlockSpec((tm, ROUTER_PAD), lambda i: (i, 0))],
        out_shape=[jax.ShapeDtypeStruct((T, D), F32),
                   jax.ShapeDtypeStruct((T, D), BF16),
                   jax.ShapeDtypeStruct((T, ROUTER_PAD), F32)],
        compiler_params=_params(("parallel",)),
        name="out_projection",
    )(y_ret, y_diff, y_gm, w_out_bf16, x2d, mod, nw, r_hi, r_lo, r_b)


def _moe_kernel(te_ref, ti_ref, nv_ref, x_ref, wgu_ref, wdn_ref, o_ref, *, hidden):
    @pl.when(pl.program_id(0) < nv_ref[0])
    def _():
        ab = jnp.dot(x_ref[...], wgu_ref[0], preferred_element_type=F32)
        hmid = (jax.nn.silu(ab[:, :hidden]) * ab[:, hidden:]).astype(BF16)
        o_ref[...] = jnp.dot(hmid, wdn_ref[0], preferred_element_type=F32).astype(o_ref.dtype)


def _expert_ffn(xb, tile_exp, tile_idx, n_valid, w_gu_bf16, w_dn_bf16, tile_m):
    P, D = xb.shape
    hidden = w_dn_bf16.shape[1]
    grid_spec = pltpu.PrefetchScalarGridSpec(
        num_scalar_prefetch=3,
        grid=(P // tile_m,),
        in_specs=[pl.BlockSpec((tile_m, D), lambda i, te, ti, nv: (ti[i], 0)),
                  pl.BlockSpec((1, D, 2 * hidden), lambda i, te, ti, nv: (te[i], 0, 0)),
                  pl.BlockSpec((1, hidden, D), lambda i, te, ti, nv: (te[i], 0, 0))],
        out_specs=pl.BlockSpec((tile_m, D), lambda i, te, ti, nv: (ti[i], 0)),
    )
    return pl.pallas_call(
        functools.partial(_moe_kernel, hidden=hidden),
        grid_spec=grid_spec,
        out_shape=jax.ShapeDtypeStruct((P, D), BF16),
        compiler_params=_params(("arbitrary",)),
        name="expert_ffn",
    )(tile_exp, tile_idx, n_valid, xb, w_gu_bf16, w_dn_bf16)


def _combine_kernel(x_ref, ya_ref, yb_ref, g_ref, mod_ref, o_ref):
    g = g_ref[...]
    y = ya_ref[...].astype(F32) * g[:, 0:1] + yb_ref[...].astype(F32) * g[:, 1:2]
    o_ref[...] = x_ref[...] + mod_ref[0, 5:6, :] * y


def _moe_combine(x1, ya, yb, gate, mod, seq):
    T, D = x1.shape
    tm = _pick(seq, (512, 256, 128))
    row = pl.BlockSpec((tm, D), lambda i: (i, 0))
    return pl.pallas_call(
        _combine_kernel,
        grid=(T // tm,),
        in_specs=[row, row, row, pl.BlockSpec((tm, TOP_K), lambda i: (i, 0)),
                  pl.BlockSpec((1, N_MOD, D), lambda i: ((i * tm) // seq, 0, 0))],
        out_specs=row,
        out_shape=jax.ShapeDtypeStruct((T, D), F32),
        compiler_params=_params(("parallel",)),
        name="moe_combine",
    )(x1, ya, yb, gate, mod)


def _route(logits, tile_m):
    T = logits.shape[0]
    gl = logits[:, :N_GROUPS]
    el = logits[:, N_GROUPS:N_GROUPS + N_EXPERTS].reshape(T, N_GROUPS, EXPERTS_PER_GROUP)
    g_idx = jnp.argmax(gl, axis=-1)
    p_g = jnp.take_along_axis(jax.nn.softmax(gl, axis=-1), g_idx[:, None], axis=-1)
    el = jnp.take_along_axis(el, g_idx[:, None, None], axis=1)[:, 0]
    top_v, top_i = lax.top_k(el, TOP_K)
    gate = p_g * jax.nn.softmax(top_v, axis=-1)
    eid = (g_idx[:, None] * EXPERTS_PER_GROUP + top_i).reshape(-1).astype(jnp.int32)
    A = T * TOP_K
    order = jnp.argsort(eid).astype(jnp.int32)
    e_s = eid[order]
    counts = jnp.sum((eid[:, None] == jnp.arange(N_EXPERTS, dtype=jnp.int32)[None, :]).astype(jnp.int32), axis=0)
    starts = jnp.cumsum(counts) - counts
    ptiles = (counts + tile_m - 1) // tile_m
    pends = jnp.cumsum(ptiles)
    pstarts = pends - ptiles
    n_tiles = -(-A // tile_m) + N_EXPERTS
    n_valid = pends[-1].astype(jnp.int32)
    tiles = jnp.arange(n_tiles, dtype=jnp.int32)
    tile_idx = jnp.minimum(tiles, n_valid - 1)
    tile_exp = jnp.minimum(jnp.sum((pends[None, :] <= tile_idx[:, None]).astype(jnp.int32), axis=1), N_EXPERTS - 1)
    slot = jnp.arange(n_tiles * tile_m, dtype=jnp.int32)
    s_exp = tile_exp[slot // tile_m]
    s_rank = slot - pstarts[s_exp] * tile_m
    s_ok = (slot // tile_m < n_valid) & (s_rank < counts[s_exp])
    s_src = jnp.clip(starts[s_exp] + s_rank, 0, A - 1)
    slot_tok = jnp.where(s_ok, order[s_src] // TOP_K, 0)
    dest_s = pstarts[e_s] * tile_m + jnp.arange(A, dtype=jnp.int32) - starts[e_s]
    dest = dest_s[jnp.argsort(order)].reshape(T, TOP_K)
    return gate, dest, slot_tok, tile_exp, tile_idx, n_valid.reshape(1)


def kernel(x, c, ada_w, ada_b, mix_norm_w, w_in, w_out, ret_norm_w, diff_q_norm_w, diff_k_norm_w, diff_lambda, diff_subln_w, gmlp_norm_w, gmlp_norm_b, gmlp_ws, gmlp_bs, ffn_norm_w, router_group_w, router_group_b, router_expert_w, router_expert_b, expert_w_gate_up, expert_w_down):
    B, S, D = x.shape
    L = ada_w.shape[0]
    T = B * S
    tile_m = 256
    mod_all = _modulation(c, ada_w, ada_b).reshape(L, B, N_MOD, D)
    x2d = x.reshape(T, D)
    for l in range(L):
        mod = mod_all[l]
        proj = _in_projection(x2d, mod, mix_norm_w[l], w_in[l].astype(BF16), S).reshape(B, S, -1)
        lam_init = 0.8 - 0.6 * math.exp(-0.3 * l)
        y_ret = _retention(proj, ret_norm_w[l])
        y_diff = _diff_attention(proj, diff_q_norm_w[l], diff_k_norm_w[l], diff_lambda[l],
                                 diff_subln_w[l], lam_init)
        y_gm = _gmlp(proj, gmlp_norm_w[l], gmlp_norm_b[l], gmlp_ws[l], gmlp_bs[l])

        r_w = jnp.concatenate([router_group_w[l], router_expert_w[l]], axis=1)
        r_w = jnp.pad(r_w, ((0, 0), (0, ROUTER_PAD - r_w.shape[1])))
        r_hi = r_w.astype(BF16)
        r_lo = (r_w - r_hi.astype(F32)).astype(BF16)
        r_b = jnp.pad(jnp.concatenate([router_group_b[l], router_expert_b[l]]),
                      (0, ROUTER_PAD - N_GROUPS - N_EXPERTS)).reshape(1, ROUTER_PAD)
        x1, h2, logits = _out_projection(
            y_ret.reshape(T, -1), y_diff.reshape(T, -1), y_gm.reshape(T, -1), w_out[l].astype(BF16),
            x2d, mod, ffn_norm_w[l], r_hi, r_lo, r_b, S)

        gate, dest, slot_tok, tile_exp, tile_idx, n_valid = _route(logits, tile_m)
        xb = h2[slot_tok]
        yb = _expert_ffn(xb, tile_exp, tile_idx, n_valid, expert_w_gate_up[l].astype(BF16),
                         expert_w_down[l].astype(BF16), tile_m)
        x2d = _moe_combine(x1, yb[dest[:, 0]], yb[dest[:, 1]], gate, mod, S)
    return x2d.reshape(B, S, D)
```

```python
import functools
import math

import jax
import jax.numpy as jnp
import numpy as np
from jax import lax
from jax.experimental import pallas as pl
from jax.experimental.pallas import tpu as pltpu

F32 = jnp.float32
BF16 = jnp.bfloat16

EPS = 1e-6
N_MOD = 6
RET_HEADS = 6
HEAD_DIM = 128
CHUNK = 128
ROPE_BASE = 10000.0
DIFF_HEADS = 6
DIFF_QK_DIM = 64
GMLP_GROUPS = 4
N_GROUPS = 4
EXPERTS_PER_GROUP = 8
N_EXPERTS = N_GROUPS * EXPERTS_PER_GROUP
TOP_K = 2
ROUTER_PAD = 128
NEG_BIG = -1e30
LOG2E = math.log2(math.e)
VMEM_LIMIT = 56 * 1024 * 1024


def _pick(n, prefs):
    for p in prefs:
        if n % p == 0:
            return p
    return n


def _params(sem, vmem=VMEM_LIMIT):
    return pltpu.CompilerParams(dimension_semantics=sem, vmem_limit_bytes=vmem)


def _mod_kernel(c_ref, w_ref, b_ref, o_ref):
    ca = jax.nn.silu(c_ref[...])
    o_ref[0] = jnp.dot(ca, w_ref[0], precision=lax.Precision.HIGHEST,
                       preferred_element_type=F32) + b_ref[0]


def _modulation(c, ada_w, ada_b):
    L, D, N = ada_w.shape
    B = c.shape[0]
    tn = _pick(N, (1024, 512, 256, 128))
    return pl.pallas_call(
        _mod_kernel,
        grid=(L, N // tn),
        in_specs=[pl.BlockSpec((B, D), lambda l, j: (0, 0)),
                  pl.BlockSpec((1, D, tn), lambda l, j: (l, 0, j)),
                  pl.BlockSpec((1, 1, tn), lambda l, j: (l, 0, j))],
        out_specs=pl.BlockSpec((1, B, tn), lambda l, j: (l, 0, j)),
        out_shape=jax.ShapeDtypeStruct((L, B, N), F32),
        compiler_params=_params(("parallel", "parallel")),
        name="modulation",
    )(c, ada_w, ada_b.reshape(L, 1, N))


def _inproj_kernel(x_ref, mod_ref, nw_ref, w_ref, o_ref, h_ref):
    @pl.when(pl.program_id(1) == 0)
    def _():
        x = x_ref[...]
        ms = jnp.mean(x * x, axis=-1, keepdims=True)
        y = x * lax.rsqrt(ms + EPS) * nw_ref[...]
        h_ref[...] = (y * (1.0 + mod_ref[0, 1:2, :]) + mod_ref[0, 0:1, :]).astype(BF16)

    o_ref[...] = jnp.dot(h_ref[...], w_ref[...], preferred_element_type=F32).astype(o_ref.dtype)


def _in_projection(x2d, mod, norm_w, w_in_bf16, seq):
    T, D = x2d.shape
    N = w_in_bf16.shape[1]
    tm = _pick(seq, (1024, 512, 256, 128))
    tn = _pick(N, (1280, 640, 128))
    return pl.pallas_call(
        _inproj_kernel,
        grid=(T // tm, N // tn),
        in_specs=[pl.BlockSpec((tm, D), lambda i, j: (i, 0)),
                  pl.BlockSpec((1, N_MOD, D), lambda i, j: ((i * tm) // seq, 0, 0)),
                  pl.BlockSpec((1, D), lambda i, j: (0, 0)),
                  pl.BlockSpec((D, tn), lambda i, j: (0, j))],
        out_specs=pl.BlockSpec((tm, tn), lambda i, j: (i, j)),
        out_shape=jax.ShapeDtypeStruct((T, N), BF16),
        scratch_shapes=[pltpu.VMEM((tm, D), BF16)],
        compiler_params=_params(("parallel", "arbitrary")),
        name="in_projection",
    )(x2d, mod, norm_w.reshape(1, D), w_in_bf16)


def _ret_kernel(q_ref, k_ref, v_ref, g_ref, cos_ref, sin_ref, dec_ref, zeta_ref, xi_ref, nw_ref,
                o_ref, state_ref, *, ts, cdec):
    @pl.when(pl.program_id(1) == 0)
    def _():
        state_ref[...] = jnp.zeros_like(state_ref)

    nt = (((1,), (1,)), ((), ()))
    tn = (((0,), (0,)), ((), ()))
    for c in range(ts // CHUNK):
        rows = slice(c * CHUNK, (c + 1) * CHUNK)
        cosb = cos_ref[rows, :]
        sinb = sin_ref[rows, :]
        for h in range(RET_HEADS):
            cols = slice(h * HEAD_DIM, (h + 1) * HEAD_DIM)
            q = q_ref[0, rows, cols].astype(F32)
            k = k_ref[0, rows, cols].astype(F32)
            v = v_ref[0, rows, cols]
            qr = q * cosb + pltpu.roll(q, HEAD_DIM // 2, 1) * sinb
            kr = (k * cosb + pltpu.roll(k, HEAD_DIM // 2, 1) * sinb) * (HEAD_DIM ** -0.5)
            qb = qr.astype(BF16)
            s = lax.dot_general(qb, kr.astype(BF16), nt, preferred_element_type=F32) * dec_ref[h]
            y = jnp.dot(s.astype(BF16), v, preferred_element_type=F32)
            st = state_ref[h]
            y = y + jnp.dot(qb, st.astype(BF16), preferred_element_type=F32) * xi_ref[h]
            kz = (kr * zeta_ref[h]).astype(BF16)
            kv = lax.dot_general(kz, v, tn, preferred_element_type=F32)
            state_ref[h] = st * cdec[h] + kv
            mu = jnp.mean(y, axis=-1, keepdims=True)
            yc = y - mu
            var = jnp.mean(yc * yc, axis=-1, keepdims=True)
            yn = yc * lax.rsqrt(var + EPS) * nw_ref[:, cols]
            g = g_ref[0, rows, cols].astype(F32)
            o_ref[0, rows, cols] = (jax.nn.silu(g) * yn).astype(o_ref.dtype)


def _retention(proj, norm_w):
    B, S, _ = proj.shape
    W = RET_HEADS * HEAD_DIM
    ts = _pick(S, (512, 256, 128))
    half = HEAD_DIM // 2
    inv = ROPE_BASE ** (-jnp.arange(half, dtype=F32) / half)
    ang = jnp.arange(S).astype(F32)[:, None] * inv[None, :]
    cos_t = jnp.concatenate([jnp.cos(ang), jnp.cos(ang)], axis=-1)
    sin_t = jnp.concatenate([-jnp.sin(ang), jnp.sin(ang)], axis=-1)
    log_g = jnp.log(1.0 - jnp.power(2.0, -5.0 - jnp.arange(RET_HEADS, dtype=F32)))
    idx = jnp.arange(CHUNK, dtype=F32)
    rel = idx[:, None] - idx[None, :]
    decay = jnp.where(rel[None] >= 0, jnp.exp(jnp.maximum(rel, 0.0)[None] * log_g[:, None, None]), 0.0)
    zeta = jnp.exp((CHUNK - 1.0 - idx)[None, :] * log_g[:, None])
    xi = jnp.exp((idx + 1.0)[None, :] * log_g[:, None])
    zeta_b = jnp.broadcast_to(zeta[:, :, None], (RET_HEADS, CHUNK, HEAD_DIM))
    xi_b = jnp.broadcast_to(xi[:, :, None], (RET_HEADS, CHUNK, HEAD_DIM))
    cdec = tuple(float(np.exp(CHUNK * np.log(1.0 - 2.0 ** (-5.0 - h)))) for h in range(RET_HEADS))

    blk = lambda j: pl.BlockSpec((1, ts, W), lambda b, s, j=j: (b, s, j))
    tab = pl.BlockSpec((RET_HEADS, CHUNK, HEAD_DIM), lambda b, s: (0, 0, 0))
    return pl.pallas_call(
        functools.partial(_ret_kernel, ts=ts, cdec=cdec),
        grid=(B, S // ts),
        in_specs=[blk(0), blk(1), blk(2), blk(3),
                  pl.BlockSpec((ts, HEAD_DIM), lambda b, s: (s, 0)),
                  pl.BlockSpec((ts, HEAD_DIM), lambda b, s: (s, 0)),
                  tab, tab, tab,
                  pl.BlockSpec((1, W), lambda b, s: (0, 0))],
        out_specs=pl.BlockSpec((1, ts, W), lambda b, s: (b, s, 0)),
        out_shape=jax.ShapeDtypeStruct((B, S, W), BF16),
        scratch_shapes=[pltpu.VMEM((RET_HEADS, HEAD_DIM, HEAD_DIM), F32)],
        compiler_params=_params(("parallel", "arbitrary")),
        name="retention",
    )(proj, proj, proj, proj, cos_t, sin_t, decay, zeta_b, xi_b, norm_w.reshape(1, W))


def _dprep_kernel(q_ref, k_ref, v_ref, qw_ref, kw_ref, ones_ref, qo_ref, ko_ref, vt_ref, *, tk):
    ones = ones_ref[...]

    def group_rms(x, w):
        sq = x * x
        hi = sq.astype(BF16)
        lo = (sq - hi.astype(F32)).astype(BF16)
        ms = (jnp.dot(hi, ones, preferred_element_type=F32)
              + jnp.dot(lo, ones, preferred_element_type=F32)) * (1.0 / DIFF_QK_DIM)
        return x * lax.rsqrt(ms + EPS) * w

    q = q_ref[0].astype(F32)
    k = k_ref[0].astype(F32)
    qo_ref[0] = (group_rms(q, qw_ref[...]) * (DIFF_QK_DIM ** -0.5 * LOG2E)).astype(qo_ref.dtype)
    ko_ref[0] = group_rms(k, kw_ref[...]).astype(ko_ref.dtype)
    ts = q.shape[0]
    for h in range(DIFF_HEADS):
        for t in range(ts // tk):
            vb = v_ref[0, t * tk:(t + 1) * tk, h * HEAD_DIM:(h + 1) * HEAD_DIM].astype(F32)
            vt_ref[0, h, t] = vb.T.astype(vt_ref.dtype)


def _diff_kernel(lam_ref, q_ref, k_ref, vt_ref, sw_ref, o_ref, qbd_ref, m_ref, l_ref, acc_ref,
                 *, tq, tk, out_scale):
    i = pl.program_id(1)
    lam = lam_ref[0, 0]
    nt = (((1,), (1,)), ((), ()))
    lane = lax.broadcasted_iota(jnp.int32, (tq, HEAD_DIM), 1)
    for h in range(DIFF_HEADS):
        q = q_ref[0, :, h * HEAD_DIM:(h + 1) * HEAD_DIM]
        zero = jnp.zeros_like(q)
        qbd_ref[h, :tq, :] = jnp.where(lane < DIFF_QK_DIM, q, zero)
        qbd_ref[h, tq:, :] = jnp.where(lane >= DIFF_QK_DIM, q, zero)
    m_ref[...] = jnp.full(m_ref.shape, NEG_BIG, F32)
    l_ref[...] = jnp.zeros(l_ref.shape, F32)
    acc_ref[...] = jnp.zeros(acc_ref.shape, F32)
    n_full = (i * tq) // tk

    def step(j, masked):
        row0 = pl.multiple_of(j * tk, tk)
        for h in range(DIFF_HEADS):
            kblk = k_ref[0, pl.ds(row0, tk), h * HEAD_DIM:(h + 1) * HEAD_DIM]
            st = lax.dot_general(kblk, qbd_ref[h], nt, preferred_element_type=F32)
            if masked:
                kpos = j * tk + lax.broadcasted_iota(jnp.int32, st.shape, 0)
                col = lax.broadcasted_iota(jnp.int32, st.shape, 1)
                qpos = i * tq + jnp.where(col >= tq, col - tq, col)
                st = jnp.where(kpos <= qpos, st, NEG_BIG)
            m = m_ref[h]
            m_new = jnp.maximum(m, jnp.max(st, axis=0, keepdims=True))
            alpha = jnp.exp2(m - m_new)
            p = jnp.exp2(st - m_new)
            m_ref[h] = m_new
            l_ref[h] = alpha * l_ref[h] + jnp.sum(p, axis=0, keepdims=True)
            acc_ref[h] = alpha * acc_ref[h] + jnp.dot(vt_ref[0, h, j], p.astype(BF16),
                                                      preferred_element_type=F32)

    def body(j, carry):
        step(j, False)
        return carry

    lax.fori_loop(0, n_full, body, 0)
    step(n_full, True)
    for h in range(DIFF_HEADS):
        acc = acc_ref[h]
        l = l_ref[h]
        o = acc[:, :tq] / l[:, :tq] - lam * (acc[:, tq:] / l[:, tq:])
        ms = jnp.mean(o * o, axis=0, keepdims=True)
        o = o * lax.rsqrt(ms + EPS) * sw_ref[...] * out_scale
        o_ref[0, :, h * HEAD_DIM:(h + 1) * HEAD_DIM] = o.T.astype(o_ref.dtype)


def _diff_attention(proj, q_norm_w, k_norm_w, lam_params, subln_w, lam_init):
    B, S, _ = proj.shape
    W = DIFF_HEADS * HEAD_DIM
    ts = _pick(S, (512, 256, 128))
    tk = ts
    tq = _pick(S, (256, 128))
    grp =jnp.arange(W) // DIFF_QK_DIM
    ones_bd = (grp[:, None] == grp[None, :]).astype(BF16)
    qw = jnp.tile(q_norm_w, W // DIFF_QK_DIM).reshape(1, W)
    kw = jnp.tile(k_norm_w, W // DIFF_QK_DIM).reshape(1, W)
    blk = lambda j: pl.BlockSpec((1, ts, W), lambda b, s, j=j: (b, s, j))
    row = pl.BlockSpec((1, W), lambda b, s: (0, 0))
    qn, kn, vt = pl.pallas_call(
        functools.partial(_dprep_kernel, tk=tk),
        grid=(B, S // ts),
        in_specs=[blk(4), blk(5), blk(6), row, row, pl.BlockSpec((W, W), lambda b, s: (0, 0))],
        out_specs=[pl.BlockSpec((1, ts, W), lambda b, s: (b, s, 0)),
                   pl.BlockSpec((1, ts, W), lambda b, s: (b, s, 0)),
                   pl.BlockSpec((1, DIFF_HEADS, ts // tk, HEAD_DIM, tk), lambda b, s: (b, 0, s, 0, 0))],
        out_shape=[jax.ShapeDtypeStruct((B, S, W), BF16),
                   jax.ShapeDtypeStruct((B, S, W), BF16),
                   jax.ShapeDtypeStruct((B, DIFF_HEADS, S // tk, HEAD_DIM, tk), BF16)],
        compiler_params=_params(("parallel", "parallel")),
        name="diff_prep",
    )(proj, proj, proj, qw, kw, ones_bd)

    lp = lam_params.astype(F32)
    lam = jnp.exp(jnp.sum(lp[0] * lp[1])) - jnp.exp(jnp.sum(lp[2] * lp[3])) + lam_init
    return pl.pallas_call(
        functools.partial(_diff_kernel, tq=tq, tk=tk, out_scale=1.0 - lam_init),
        grid=(B, S // tq),
        in_specs=[pl.BlockSpec(memory_space=pltpu.SMEM),
                  pl.BlockSpec((1, tq, W), lambda b, i: (b, i, 0)),
                  pl.BlockSpec((1, S, W), lambda b, i: (b, 0, 0)),
                  pl.BlockSpec((1, DIFF_HEADS, S // tk, HEAD_DIM, tk), lambda b, i: (b, 0, 0, 0, 0)),
                  pl.BlockSpec((HEAD_DIM, 1), lambda b, i: (0, 0))],
        out_specs=pl.BlockSpec((1, tq, W), lambda b, i: (b, i, 0)),
        out_shape=jax.ShapeDtypeStruct((B, S, W), BF16),
        scratch_shapes=[pltpu.VMEM((DIFF_HEADS, 2 * tq, HEAD_DIM), BF16),
                        pltpu.VMEM((DIFF_HEADS, 1, 2 * tq), F32),
                        pltpu.VMEM((DIFF_HEADS, 1, 2 * tq), F32),
                        pltpu.VMEM((DIFF_HEADS, HEAD_DIM, 2 * tq), F32)],
        compiler_params=_params(("parallel", "parallel")),
        name="diff_attention",
    )(lam.reshape(1, 1), qn, kn, vt, subln_w.reshape(HEAD_DIM, 1))


def _gelu_exact(x):
    return 0.5 * x * (1.0 + lax.erf(x * (2.0 ** -0.5)))


def _gmlp_kernel(ua_ref, ub_ref, va_ref, vb_ref, nw_ref, nb_ref, ws_ref, bs_ref, o_ref, *, ts):
    u = jnp.concatenate([ua_ref[0], ub_ref[0]], axis=-1).astype(F32)
    v = jnp.concatenate([va_ref[0], vb_ref[0]], axis=-1).astype(F32)
    u = _gelu_exact(u)
    v = _gelu_exact(v)
    mu = jnp.mean(v, axis=-1, keepdims=True)
    vc = v - mu
    var = jnp.mean(vc * vc, axis=-1, keepdims=True)
    v = (vc * lax.rsqrt(var + EPS) * nw_ref[...] + nb_ref[...]).astype(BF16)
    r = lax.broadcasted_iota(jnp.int32, (CHUNK, CHUNK), 0)
    cc = lax.broadcasted_iota(jnp.int32, (CHUNK, CHUNK), 1)
    for g in range(GMLP_GROUPS):
        cols = slice(g * HEAD_DIM, (g + 1) * HEAD_DIM)
        w = jnp.where(r >= cc, ws_ref[g], 0.0).astype(BF16)
        for c in range(ts // CHUNK):
            rows = slice(c * CHUNK, (c + 1) * CHUNK)
            f = jnp.dot(w, v[rows, cols], preferred_element_type=F32) + bs_ref[g]
            o_ref[0, rows, cols] = (u[rows, cols] * f).astype(o_ref.dtype)


def _gmlp(proj, norm_w, norm_b, ws, bs):
    B, S, N = proj.shape
    W = GMLP_GROUPS * HEAD_DIM
    ts = _pick(S, (512, 256, 128))
    half = W // 2
    base = (N - 2 * W) // half
    blk = lambda j: pl.BlockSpec((1, ts, half), lambda b, s, j=j: (b, s, base + j))
    row = pl.BlockSpec((1, W), lambda b, s: (0, 0))
    tab = pl.BlockSpec((GMLP_GROUPS, CHUNK, CHUNK), lambda b, s: (0, 0, 0))
    bs_b = jnp.broadcast_to(bs[:, :, None], (GMLP_GROUPS, CHUNK, HEAD_DIM))
    return pl.pallas_call(
        functools.partial(_gmlp_kernel, ts=ts),
        grid=(B, S // ts),
        in_specs=[blk(0), blk(1), blk(2), blk(3), row, row, tab, tab],
        out_specs=pl.BlockSpec((1, ts, W), lambda b, s: (b, s, 0)),
        out_shape=jax.ShapeDtypeStruct((B, S, W), BF16),
        compiler_params=_params(("parallel", "parallel")),
        name="gmlp",
    )(proj, proj, proj, proj, norm_w.reshape(1, W), norm_b.reshape(1, W), ws, bs_b)


def _outproj_kernel(yr_ref, yd_ref, yg_ref, w_ref, x_ref, mod_ref, nw_ref, rhi_ref, rlo_ref, rb_ref,
                    x1_ref, h2_ref, lg_ref):
    cat = jnp.concatenate([yr_ref[...], yd_ref[...], yg_ref[...]], axis=-1)
    mix = jnp.dot(cat, w_ref[...], preferred_element_type=F32)
    x1 = x_ref[...] + mod_ref[0, 2:3, :] * mix
    x1_ref[...] = x1
    ms = jnp.mean(x1 * x1, axis=-1, keepdims=True)
    h2 = x1 * lax.rsqrt(ms + EPS) * nw_ref[...]
    h2 = h2 * (1.0 + mod_ref[0, 4:5, :]) + mod_ref[0, 3:4, :]
    hi = h2.astype(BF16)
    lo = (h2 - hi.astype(F32)).astype(BF16)
    h2_ref[...] = h2
    rhi = rhi_ref[...]
    lg_ref[...] = (jnp.dot(hi, rhi, preferred_element_type=F32)
                   + jnp.dot(lo, rhi, preferred_element_type=F32)
                   + jnp.dot(hi, rlo_ref[...], preferred_element_type=F32)) + rb_ref[...]


def _out_projection(y_ret, y_diff, y_gm, w_out_bf16, x2d, mod, norm_w, r_hi, r_lo, r_b, seq):
    T, D = x2d.shape
    tm = _pick(seq, (512, 256, 128))
    rowblk = lambda a: pl.BlockSpec((tm, a.shape[1]), lambda i: (i, 0))
    full = lambda a: pl.BlockSpec(a.shape, lambda i: (0, 0))
    nw = norm_w.reshape(1, D)
    return pl.pallas_call(
        _outproj_kernel,
        grid=(T // tm,),
        in_specs=[rowblk(y_ret), rowblk(y_diff), rowblk(y_gm), full(w_out_bf16), rowblk(x2d),
                  pl.BlockSpec((1, N_MOD, D), lambda i: ((i * tm) // seq, 0, 0)),
                  full(nw), full(r_hi), full(r_lo), full(r_b)],
        out_specs=[pl.BlockSpec((tm, D), lambda i: (i, 0)),
                   pl.BlockSpec((tm, D), lambda i: (i, 0)),
                   pl.BlockSpec((tm, ROUTER_PAD), lambda i: (i, 0))],
        out_shape=[jax.ShapeDtypeStruct((T, D), F32),
                   jax.ShapeDtypeStruct((T, D), F32),
                   jax.ShapeDtypeStruct((T, ROUTER_PAD), F32)],
        compiler_params=_params(("parallel",)),
        name="out_projection",
    )(y_ret, y_diff, y_gm, w_out_bf16, x2d, mod, nw, r_hi, r_lo, r_b)


def _row_gather_start(tok_ref, src_hbm, dst_buf, sem, slot, tile_m):
    for r in range(tile_m):
        pltpu.make_async_copy(src_hbm.at[pl.ds(tok_ref[0, 0, r], 1), :],
                              dst_buf.at[slot, pl.ds(r, 1), :], sem.at[slot]).start()


def _moe_up_kernel(te_ref, ti_ref, nv_ref, tokc_ref, tokn_ref, h_hbm, wgu_ref, o_ref,
                   wbf_ref, xbuf_ref, gsem, *, hidden, tile_m):
    i = pl.program_id(0)
    nv = nv_ref[0]

    @pl.when(i < nv)
    def _():
        slot = i % 2

        @pl.when(i == 0)
        def _():
            _row_gather_start(tokc_ref, h_hbm, xbuf_ref, gsem, 0, tile_m)

        @pl.when(i + 1 < nv)
        def _():
            _row_gather_start(tokn_ref, h_hbm, xbuf_ref, gsem, 1 - slot, tile_m)

        @pl.when((i == 0) | (te_ref[i] != te_ref[jnp.maximum(i - 1, 0)]))
        def _():
            wbf_ref[...] = wgu_ref[0].astype(BF16)

        pltpu.make_async_copy(h_hbm.at[pl.ds(0, tile_m), :], xbuf_ref.at[slot], gsem.at[slot]).wait()
        ab = jnp.dot(xbuf_ref[slot].astype(BF16), wbf_ref[...], preferred_element_type=F32)
        o_ref[...] = (jax.nn.silu(ab[:, :hidden]) * ab[:, hidden:]).astype(o_ref.dtype)

    @pl.when(i >= nv)
    def _():
        o_ref[...] = jnp.zeros_like(o_ref)


def _moe_down_kernel(te_ref, ti_ref, nv_ref, dst_ref, hm_ref, wdn_ref, g_ref, y_hbm,
                     wbf_ref, ybuf_ref, ssem, *, tile_m, y_rows):
    i = pl.program_id(0)
    nv = nv_ref[0]

    def scatter_wait(slot):
        pltpu.make_async_copy(ybuf_ref.at[slot], y_hbm.at[pl.ds(0, tile_m), :], ssem.at[slot]).wait()

    @pl.when(i == 0)
    def _():
        ybuf_ref[...] = jnp.zeros_like(ybuf_ref)
        spare0 = y_rows - 2 * tile_m
        fills = [pltpu.make_async_copy(ybuf_ref.at[s], y_hbm.at[pl.ds(k * y_rows + spare0 + s * tile_m, tile_m), :],
                                       ssem.at[s]) for k in range(TOP_K) for s in range(2)]
        for cp in fills:
            cp.start()
        for cp in fills:
            cp.wait()

    @pl.when(i < nv)
    def _():
        slot = i % 2

        @pl.when((i == 0) | (te_ref[i] != te_ref[jnp.maximum(i - 1, 0)]))
        def _():
            wbf_ref[...] = wdn_ref[0].astype(BF16)

        @pl.when(i >= 2)
        def _():
            scatter_wait(slot)

        ybuf_ref[slot] = jnp.dot(hm_ref[...], wbf_ref[...], preferred_element_type=F32) * g_ref[...]
        for r in range(tile_m):
            pltpu.make_async_copy(ybuf_ref.at[slot, pl.ds(r, 1), :],
                                  y_hbm.at[pl.ds(dst_ref[0, 0, r], 1), :], ssem.at[slot]).start()

        @pl.when(i == nv - 1)
        def _():
            scatter_wait(slot)

            @pl.when(i >= 1)
            def _():
                scatter_wait(1 - slot)


def _expert_ffn(h2, slot_tok, slot_dst, slot_gate, tile_exp, tile_idx, n_valid, w_gu, w_dn, tile_m, y_rows):
    T, D = h2.shape
    hidden = w_dn.shape[1]
    n_tiles = tile_exp.shape[0]
    P = n_tiles * tile_m
    tok3 = slot_tok.reshape(n_tiles, 1, tile_m)
    dst3 = slot_dst.reshape(n_tiles, 1, tile_m)
    smem_blk = lambda f: pl.BlockSpec((1, 1, tile_m), f, memory_space=pltpu.SMEM)
    hmid = pl.pallas_call(
        functools.partial(_moe_up_kernel, hidden=hidden, tile_m=tile_m),
        grid_spec=pltpu.PrefetchScalarGridSpec(
            num_scalar_prefetch=3,
            grid=(n_tiles,),
            in_specs=[smem_blk(lambda i, te, ti, nv: (i, 0, 0)),
                      smem_blk(lambda i, te, ti, nv: (jnp.minimum(i + 1, n_tiles - 1), 0, 0)),
                      pl.BlockSpec(memory_space=pl.ANY),
                      pl.BlockSpec((1, D, 2 * hidden), lambda i, te, ti, nv: (te[i], 0, 0))],
            out_specs=pl.BlockSpec((tile_m, hidden), lambda i, te, ti, nv: (i, 0)),
            scratch_shapes=[pltpu.VMEM((D, 2 * hidden), BF16),
                            pltpu.VMEM((2, tile_m, D), F32),
                            pltpu.SemaphoreType.DMA((2,))]),
        out_shape=jax.ShapeDtypeStruct((P, hidden), BF16),
        compiler_params=_params(("arbitrary",)),
        name="expert_up",
    )(tile_exp, tile_idx, n_valid, tok3, tok3, h2, w_gu)
    return pl.pallas_call(
        functools.partial(_moe_down_kernel, tile_m=tile_m, y_rows=y_rows),
        grid_spec=pltpu.PrefetchScalarGridSpec(
            num_scalar_prefetch=3,
            grid=(n_tiles,),
            in_specs=[smem_blk(lambda i, te, ti, nv: (i, 0, 0)),
                      pl.BlockSpec((tile_m, hidden), lambda i, te, ti, nv: (ti[i], 0)),
                      pl.BlockSpec((1, hidden, D), lambda i, te, ti, nv: (te[i], 0, 0)),
                      pl.BlockSpec((tile_m, 1), lambda i, te, ti, nv: (ti[i], 0))],
            out_specs=pl.BlockSpec(memory_space=pl.ANY),
            scratch_shapes=[pltpu.VMEM((hidden, D), BF16),
                            pltpu.VMEM((2, tile_m, D), F32),
                            pltpu.SemaphoreType.DMA((2,))]),
        out_shape=jax.ShapeDtypeStruct((TOP_K * y_rows, D), F32),
        compiler_params=_params(("arbitrary",)),
        name="expert_down",
    )(tile_exp, tile_idx, n_valid, dst3, hmid, w_dn, slot_gate.reshape(P, 1))


def _combine_kernel(x_ref, ya_ref, yb_ref, mod_ref, o_ref):
    o_ref[...] = x_ref[...] + mod_ref[0, 5:6, :] * (ya_ref[0] + yb_ref[0])


def _moe_combine(x1, y2, mod, seq):
    T, D = x1.shape
    tm = _pick(seq, (512, 256, 128))
    row = pl.BlockSpec((tm, D), lambda i: (i, 0))
    return pl.pallas_call(
        _combine_kernel,
        grid=(T // tm,),
        in_specs=[row,
                  pl.BlockSpec((1, tm, D), lambda i: (0, i, 0)),
                  pl.BlockSpec((1, tm, D), lambda i: (1, i, 0)),
                  pl.BlockSpec((1, N_MOD, D), lambda i: ((i * tm) // seq, 0, 0))],
        out_specs=row,
        out_shape=jax.ShapeDtypeStruct((T, D), F32),
        compiler_params=_params(("parallel",)),
        name="moe_combine",
    )(x1, y2, y2, mod)


def _route(logits, tile_m, y_rows):
    T = logits.shape[0]
    gl = logits[:, :N_GROUPS]
    el = logits[:, N_GROUPS:N_GROUPS + N_EXPERTS].reshape(T, N_GROUPS, EXPERTS_PER_GROUP)
    g_idx = jnp.argmax(gl, axis=-1)
    p_g = jnp.take_along_axis(jax.nn.softmax(gl, axis=-1), g_idx[:, None], axis=-1)
    el = jnp.take_along_axis(el, g_idx[:, None, None], axis=1)[:, 0]
    top_v, top_i = lax.top_k(el, TOP_K)
    gate = (p_g * jax.nn.softmax(top_v, axis=-1)).reshape(-1)
    eid = (g_idx[:, None] * EXPERTS_PER_GROUP + top_i).reshape(-1).astype(jnp.int32)
    A = T * TOP_K
    order = jnp.argsort(eid).astype(jnp.int32)
    counts = jnp.sum((eid[:, None] == jnp.arange(N_EXPERTS, dtype=jnp.int32)[None, :]).astype(jnp.int32), axis=0)
    starts = jnp.cumsum(counts) - counts
    ptiles = (counts + tile_m - 1) // tile_m
    pends = jnp.cumsum(ptiles)
    pstarts = pends - ptiles
    n_tiles = -(-A // tile_m) + N_EXPERTS
    n_valid = pends[-1].astype(jnp.int32)
    tiles = jnp.arange(n_tiles, dtype=jnp.int32)
    tile_idx = jnp.minimum(tiles, n_valid - 1)
    tile_exp = jnp.minimum(jnp.sum((pends[None, :] <= tile_idx[:, None]).astype(jnp.int32), axis=1), N_EXPERTS - 1)
    slot = jnp.arange(n_tiles * tile_m, dtype=jnp.int32)
    s_tile = slot // tile_m
    s_exp = tile_exp[s_tile]
    s_rank = slot - pstarts[s_exp] * tile_m
    s_ok = (s_tile < n_valid) & (s_rank < counts[s_exp])
    s_asg = order[jnp.clip(starts[s_exp] + s_rank, 0, A - 1)]
    slot_tok = jnp.where(s_ok, s_asg // TOP_K, 0)
    spare = T + (s_tile % 2) * tile_m + slot % tile_m
    slot_dst = jnp.where(s_ok, (s_asg % TOP_K) * y_rows + s_asg // TOP_K, spare)
    slot_gate = jnp.where(s_ok, gate[s_asg], 0.0)
    return slot_tok, slot_dst, slot_gate, tile_exp, tile_idx, n_valid.reshape(1)


def kernel(x, c, ada_w, ada_b, mix_norm_w, w_in, w_out, ret_norm_w, diff_q_norm_w, diff_k_norm_w, diff_lambda, diff_subln_w, gmlp_norm_w, gmlp_norm_b, gmlp_ws, gmlp_bs, ffn_norm_w, router_group_w, router_group_b, router_expert_w, router_expert_b, expert_w_gate_up, expert_w_down):
    B, S, D = x.shape
    L = ada_w.shape[0]
    T = B * S
    tile_m = 256
    y_rows = T + 2 * tile_m
    mod_all =_modulation(c, ada_w, ada_b).reshape(L, B, N_MOD, D)
    x2d = x.reshape(T, D)
    for l in range(L):
        mod = mod_all[l]
        proj = _in_projection(x2d, mod, mix_norm_w[l], w_in[l].astype(BF16), S).reshape(B, S, -1)
        lam_init = 0.8 - 0.6 * math.exp(-0.3 * l)
        y_ret = _retention(proj, ret_norm_w[l])
        y_diff = _diff_attention(proj, diff_q_norm_w[l], diff_k_norm_w[l], diff_lambda[l],
                                 diff_subln_w[l], lam_init)
        y_gm = _gmlp(proj, gmlp_norm_w[l], gmlp_norm_b[l], gmlp_ws[l], gmlp_bs[l])

        r_w = jnp.concatenate([router_group_w[l], router_expert_w[l]], axis=1)
        r_w = jnp.pad(r_w, ((0, 0), (0, ROUTER_PAD - r_w.shape[1])))
        r_hi = r_w.astype(BF16)
        r_lo = (r_w - r_hi.astype(F32)).astype(BF16)
        r_b = jnp.pad(jnp.concatenate([router_group_b[l], router_expert_b[l]]),
                      (0, ROUTER_PAD - N_GROUPS - N_EXPERTS)).reshape(1, ROUTER_PAD)
        x1, h2, logits = _out_projection(
            y_ret.reshape(T, -1), y_diff.reshape(T, -1), y_gm.reshape(T, -1), w_out[l].astype(BF16),
            x2d, mod, ffn_norm_w[l], r_hi, r_lo, r_b, S)

        slot_tok, slot_dst, slot_gate, tile_exp, tile_idx, n_valid = _route(logits, tile_m, y_rows)
        y2 = _expert_ffn(h2, slot_tok, slot_dst, slot_gate, tile_exp, tile_idx, n_valid,
                         expert_w_gate_up[l], expert_w_down[l], tile_m, y_rows)
        x2d = _moe_combine(x1, y2.reshape(TOP_K, y_rows, D), mod, S)
    return x2d.reshape(B, S, D)
```

```python
import functools
import math

import jax
import jax.numpy as jnp
import numpy as np
from jax import lax
from jax.experimental import pallas as pl
from jax.experimental.pallas import tpu as pltpu

F32 = jnp.float32
BF16 = jnp.bfloat16

EPS = 1e-6
N_MOD = 6
RET_HEADS = 6
HEAD_DIM = 128
CHUNK = 128
ROPE_BASE = 10000.0
DIFF_HEADS = 6
DIFF_QK_DIM = 64
GMLP_GROUPS = 4
N_GROUPS = 4
EXPERTS_PER_GROUP = 8
N_EXPERTS = N_GROUPS * EXPERTS_PER_GROUP
TOP_K = 2
ROUTER_PAD = 128
NEG_BIG = -1e30
LOG2E = math.log2(math.e)
VMEM_LIMIT = 56 * 1024 * 1024


def _pick(n, prefs):
    for p in prefs:
        if n % p == 0:
            return p
    return n


def _params(sem, vmem=VMEM_LIMIT):
    return pltpu.CompilerParams(dimension_semantics=sem, vmem_limit_bytes=vmem)


def _mod_kernel(c_ref, w_ref, b_ref, o_ref):
    ca = jax.nn.silu(c_ref[...])
    o_ref[0] = jnp.dot(ca, w_ref[0], precision=lax.Precision.HIGHEST,
                       preferred_element_type=F32) + b_ref[0]


def _modulation(c, ada_w, ada_b):
    L, D, N = ada_w.shape
    B = c.shape[0]
    tn = _pick(N, (1024, 512, 256, 128))
    return pl.pallas_call(
        _mod_kernel,
        grid=(L, N // tn),
        in_specs=[pl.BlockSpec((B, D), lambda l, j: (0, 0)),
                  pl.BlockSpec((1, D, tn), lambda l, j: (l, 0, j)),
                  pl.BlockSpec((1, 1, tn), lambda l, j: (l, 0, j))],
        out_specs=pl.BlockSpec((1, B, tn), lambda l, j: (l, 0, j)),
        out_shape=jax.ShapeDtypeStruct((L, B, N), F32),
        compiler_params=_params(("parallel", "parallel")),
        name="modulation",
    )(c, ada_w, ada_b.reshape(L, 1, N))


def _inproj_kernel(x_ref, mod_ref, nw_ref, w_ref, o_ref, h_ref):
    @pl.when(pl.program_id(1) == 0)
    def _():
        x = x_ref[...]
        ms = jnp.mean(x * x, axis=-1, keepdims=True)
        y = x * lax.rsqrt(ms + EPS) * nw_ref[...]
        h_ref[...] = (y * (1.0 + mod_ref[0, 1:2, :]) + mod_ref[0, 0:1, :]).astype(BF16)

    o_ref[...] = jnp.dot(h_ref[...], w_ref[...], preferred_element_type=F32).astype(o_ref.dtype)


def _in_projection(x2d, mod, norm_w, w_in_bf16, seq):
    T, D = x2d.shape
    N = w_in_bf16.shape[1]
    tm = _pick(seq, (1024, 512, 256, 128))
    tn = _pick(N, (1280, 640, 128))
    return pl.pallas_call(
        _inproj_kernel,
        grid=(T // tm, N // tn),
        in_specs=[pl.BlockSpec((tm, D), lambda i, j: (i, 0)),
                  pl.BlockSpec((1, N_MOD, D), lambda i, j: ((i * tm) // seq, 0, 0)),
                  pl.BlockSpec((1, D), lambda i, j: (0, 0)),
                  pl.BlockSpec((D, tn), lambda i, j: (0, j))],
        out_specs=pl.BlockSpec((tm, tn), lambda i, j: (i, j)),
        out_shape=jax.ShapeDtypeStruct((T, N), BF16),
        scratch_shapes=[pltpu.VMEM((tm, D), BF16)],
        compiler_params=_params(("parallel", "arbitrary")),
        name="in_projection",
    )(x2d, mod, norm_w.reshape(1, D), w_in_bf16)


def _ret_kernel(q_ref, k_ref, v_ref, g_ref, cos_ref, sin_ref, dec_ref, zeta_ref, xi_ref, nw_ref,
                o_ref, state_ref, *, ts, cdec):
    @pl.when(pl.program_id(1) == 0)
    def _():
        state_ref[...] = jnp.zeros_like(state_ref)

    nt = (((1,), (1,)), ((), ()))
    tn = (((0,), (0,)), ((), ()))
    for c in range(ts // CHUNK):
        rows = slice(c * CHUNK, (c + 1) * CHUNK)
        cosb = cos_ref[rows, :]
        sinb = sin_ref[rows, :]
        for h in range(RET_HEADS):
            cols = slice(h * HEAD_DIM, (h + 1) * HEAD_DIM)
            q = q_ref[0, rows, cols].astype(F32)
            k = k_ref[0, rows, cols].astype(F32)
            v = v_ref[0, rows, cols]
            qr = q * cosb + pltpu.roll(q, HEAD_DIM // 2, 1) * sinb
            kr = (k * cosb + pltpu.roll(k, HEAD_DIM // 2, 1) * sinb) * (HEAD_DIM ** -0.5)
            qb = qr.astype(BF16)
            s = lax.dot_general(qb, kr.astype(BF16), nt, preferred_element_type=F32) * dec_ref[h]
            y = jnp.dot(s.astype(BF16), v, preferred_element_type=F32)
            st = state_ref[h]
            y = y + jnp.dot(qb, st.astype(BF16), preferred_element_type=F32) * xi_ref[h]
            kz = (kr * zeta_ref[h]).astype(BF16)
            kv = lax.dot_general(kz, v, tn, preferred_element_type=F32)
            state_ref[h] = st * cdec[h] + kv
            mu = jnp.mean(y, axis=-1, keepdims=True)
            yc = y - mu
            var = jnp.mean(yc * yc, axis=-1, keepdims=True)
            yn = yc * lax.rsqrt(var + EPS) * nw_ref[:, cols]
            g = g_ref[0, rows, cols].astype(F32)
            o_ref[0, rows, cols] = (jax.nn.silu(g) * yn).astype(o_ref.dtype)


def _retention(proj, norm_w):
    B, S, _ = proj.shape
    W = RET_HEADS * HEAD_DIM
    ts = _pick(S, (512, 256, 128))
    half = HEAD_DIM // 2
    inv = ROPE_BASE ** (-jnp.arange(half, dtype=F32) / half)
    ang = jnp.arange(S).astype(F32)[:, None] * inv[None, :]
    cos_t = jnp.concatenate([jnp.cos(ang), jnp.cos(ang)], axis=-1)
    sin_t = jnp.concatenate([-jnp.sin(ang), jnp.sin(ang)], axis=-1)
    log_g = jnp.log(1.0 - jnp.power(2.0, -5.0 - jnp.arange(RET_HEADS, dtype=F32)))
    idx = jnp.arange(CHUNK, dtype=F32)
    rel = idx[:, None] - idx[None, :]
    decay = jnp.where(rel[None] >= 0, jnp.exp(jnp.maximum(rel, 0.0)[None] * log_g[:, None, None]), 0.0)
    zeta = jnp.exp((CHUNK - 1.0 - idx)[None, :] * log_g[:, None])
    xi = jnp.exp((idx + 1.0)[None, :] * log_g[:, None])
    zeta_b = jnp.broadcast_to(zeta[:, :, None], (RET_HEADS, CHUNK, HEAD_DIM))
    xi_b = jnp.broadcast_to(xi[:, :, None], (RET_HEADS, CHUNK, HEAD_DIM))
    cdec = tuple(float(np.exp(CHUNK * np.log(1.0 - 2.0 ** (-5.0 - h)))) for h in range(RET_HEADS))

    blk = lambda j: pl.BlockSpec((1, ts, W), lambda b, s, j=j: (b, s, j))
    tab = pl.BlockSpec((RET_HEADS, CHUNK, HEAD_DIM), lambda b, s: (0, 0, 0))
    return pl.pallas_call(
        functools.partial(_ret_kernel, ts=ts, cdec=cdec),
        grid=(B, S // ts),
        in_specs=[blk(0), blk(1), blk(2), blk(3),
                  pl.BlockSpec((ts, HEAD_DIM), lambda b, s: (s, 0)),
                  pl.BlockSpec((ts, HEAD_DIM), lambda b, s: (s, 0)),
                  tab, tab, tab,
                  pl.BlockSpec((1, W), lambda b, s: (0, 0))],
        out_specs=pl.BlockSpec((1, ts, W), lambda b, s: (b, s, 0)),
        out_shape=jax.ShapeDtypeStruct((B, S, W), BF16),
        scratch_shapes=[pltpu.VMEM((RET_HEADS, HEAD_DIM, HEAD_DIM), F32)],
        compiler_params=_params(("parallel", "arbitrary")),
        name="retention",
    )(proj, proj, proj, proj, cos_t, sin_t, decay, zeta_b, xi_b, norm_w.reshape(1, W))


def _dprep_kernel(q_ref, k_ref, v_ref, qw_ref, kw_ref, ones_ref, qo_ref, ko_ref, vt_ref, *, tk):
    ones = ones_ref[...]

    def group_rms(x, w):
        sq = x * x
        hi = sq.astype(BF16)
        lo = (sq - hi.astype(F32)).astype(BF16)
        ms = (jnp.dot(hi, ones, preferred_element_type=F32)
              + jnp.dot(lo, ones, preferred_element_type=F32)) * (1.0 / DIFF_QK_DIM)
        return x * lax.rsqrt(ms + EPS) * w

    q = q_ref[0].astype(F32)
    k = k_ref[0].astype(F32)
    qo_ref[0] = (group_rms(q, qw_ref[...]) * (DIFF_QK_DIM ** -0.5 * LOG2E)).astype(qo_ref.dtype)
    ko_ref[0] = group_rms(k, kw_ref[...]).astype(ko_ref.dtype)
    ts = q.shape[0]
    for h in range(DIFF_HEADS):
        for t in range(ts // tk):
            vb = v_ref[0, t * tk:(t + 1) * tk, h * HEAD_DIM:(h + 1) * HEAD_DIM].astype(F32)
            vt_ref[0, h, t] = vb.T.astype(vt_ref.dtype)


def _diff_kernel(lam_ref, q_ref, k_ref, vt_ref, sw_ref, o_ref, qbd_ref, m_ref, l_ref, acc_ref,
                 *, tq, tk, out_scale):
    i = pl.program_id(1)
    lam = lam_ref[0, 0]
    nt = (((1,), (1,)), ((), ()))
    lane = lax.broadcasted_iota(jnp.int32, (tq, HEAD_DIM), 1)
    for h in range(DIFF_HEADS):
        q = q_ref[0, :, h * HEAD_DIM:(h + 1) * HEAD_DIM]
        zero = jnp.zeros_like(q)
        qbd_ref[h, :tq, :] = jnp.where(lane < DIFF_QK_DIM, q, zero)
        qbd_ref[h, tq:, :] = jnp.where(lane >= DIFF_QK_DIM, q, zero)
    m_ref[...] = jnp.full(m_ref.shape, NEG_BIG, F32)
    l_ref[...] = jnp.zeros(l_ref.shape, F32)
    acc_ref[...] = jnp.zeros(acc_ref.shape, F32)
    n_full = (i * tq) // tk

    def step(j, masked):
        row0 = pl.multiple_of(j * tk, tk)
        for h in range(DIFF_HEADS):
            kblk = k_ref[0, pl.ds(row0, tk), h * HEAD_DIM:(h + 1) * HEAD_DIM]
            st = lax.dot_general(kblk, qbd_ref[h], nt, preferred_element_type=F32)
            if masked:
                kpos = j * tk + lax.broadcasted_iota(jnp.int32, st.shape, 0)
                col = lax.broadcasted_iota(jnp.int32, st.shape, 1)
                qpos = i * tq + jnp.where(col >= tq, col - tq, col)
                st = jnp.where(kpos <= qpos, st, NEG_BIG)
            m = m_ref[h]
            m_new = jnp.maximum(m, jnp.max(st, axis=0, keepdims=True))
            alpha = jnp.exp2(m - m_new)
            p = jnp.exp2(st - m_new)
            m_ref[h] = m_new
            l_ref[h] = alpha * l_ref[h] + jnp.sum(p, axis=0, keepdims=True)
            acc_ref[h] = alpha * acc_ref[h] + jnp.dot(vt_ref[0, h, j], p.astype(BF16),
                                                      preferred_element_type=F32)

    def body(j, carry):
        step(j, False)
        return carry

    lax.fori_loop(0, n_full, body, 0)
    step(n_full, True)
    for h in range(DIFF_HEADS):
        acc = acc_ref[h]
        l = l_ref[h]
        o = acc[:, :tq] / l[:, :tq] - lam * (acc[:, tq:] / l[:, tq:])
        ms = jnp.mean(o * o, axis=0, keepdims=True)
        o = o * lax.rsqrt(ms + EPS) * sw_ref[...] * out_scale
        o_ref[0, :, h * HEAD_DIM:(h + 1) * HEAD_DIM] = o.T.astype(o_ref.dtype)


def _diff_attention(proj, q_norm_w, k_norm_w, lam_params, subln_w, lam_init):
    B, S, _ = proj.shape
    W = DIFF_HEADS * HEAD_DIM
    ts = _pick(S, (512, 256, 128))
    tk = ts
    tq = _pick(S, (256, 128))
    grp =jnp.arange(W) // DIFF_QK_DIM
    ones_bd = (grp[:, None] == grp[None, :]).astype(BF16)
    qw = jnp.tile(q_norm_w, W // DIFF_QK_DIM).reshape(1, W)
    kw = jnp.tile(k_norm_w, W // DIFF_QK_DIM).reshape(1, W)
    blk = lambda j: pl.BlockSpec((1, ts, W), lambda b, s, j=j: (b, s, j))
    row = pl.BlockSpec((1, W), lambda b, s: (0, 0))
    qn, kn, vt = pl.pallas_call(
        functools.partial(_dprep_kernel, tk=tk),
        grid=(B, S // ts),
        in_specs=[blk(4), blk(5), blk(6), row, row, pl.BlockSpec((W, W), lambda b, s: (0, 0))],
        out_specs=[pl.BlockSpec((1, ts, W), lambda b, s: (b, s, 0)),
                   pl.BlockSpec((1, ts, W), lambda b, s: (b, s, 0)),
                   pl.BlockSpec((1, DIFF_HEADS, ts // tk, HEAD_DIM, tk), lambda b, s: (b, 0, s, 0, 0))],
        out_shape=[jax.ShapeDtypeStruct((B, S, W), BF16),
                   jax.ShapeDtypeStruct((B, S, W), BF16),
                   jax.ShapeDtypeStruct((B, DIFF_HEADS, S // tk, HEAD_DIM, tk), BF16)],
        compiler_params=_params(("parallel", "parallel")),
        name="diff_prep",
    )(proj, proj, proj, qw, kw, ones_bd)

    lp = lam_params.astype(F32)
    lam = jnp.exp(jnp.sum(lp[0] * lp[1])) - jnp.exp(jnp.sum(lp[2] * lp[3])) + lam_init
    return pl.pallas_call(
        functools.partial(_diff_kernel, tq=tq, tk=tk, out_scale=1.0 - lam_init),
        grid=(B, S // tq),
        in_specs=[pl.BlockSpec(memory_space=pltpu.SMEM),
                  pl.BlockSpec((1, tq, W), lambda b, i: (b, i, 0)),
                  pl.BlockSpec((1, S, W), lambda b, i: (b, 0, 0)),
                  pl.BlockSpec((1, DIFF_HEADS, S // tk, HEAD_DIM, tk), lambda b, i: (b, 0, 0, 0, 0)),
                  pl.BlockSpec((HEAD_DIM, 1), lambda b, i: (0, 0))],
        out_specs=pl.BlockSpec((1, tq, W), lambda b, i: (b, i, 0)),
        out_shape=jax.ShapeDtypeStruct((B, S, W), BF16),
        scratch_shapes=[pltpu.VMEM((DIFF_HEADS, 2 * tq, HEAD_DIM), BF16),
                        pltpu.VMEM((DIFF_HEADS, 1, 2 * tq), F32),
                        pltpu.VMEM((DIFF_HEADS, 1, 2 * tq), F32),
                        pltpu.VMEM((DIFF_HEADS, HEAD_DIM, 2 * tq), F32)],
        compiler_params=_params(("parallel", "parallel")),
        name="diff_attention",
    )(lam.reshape(1, 1), qn, kn, vt, subln_w.reshape(HEAD_DIM, 1))


def _gelu_exact(x):
    return 0.5 * x * (1.0 + lax.erf(x * (2.0 ** -0.5)))


def _gmlp_kernel(ua_ref, ub_ref, va_ref, vb_ref, nw_ref, nb_ref, ws_ref, bs_ref, o_ref, *, ts):
    u = jnp.concatenate([ua_ref[0], ub_ref[0]], axis=-1).astype(F32)
    v = jnp.concatenate([va_ref[0], vb_ref[0]], axis=-1).astype(F32)
    u = _gelu_exact(u)
    v = _gelu_exact(v)
    mu = jnp.mean(v, axis=-1, keepdims=True)
    vc = v - mu
    var = jnp.mean(vc * vc, axis=-1, keepdims=True)
    v = (vc * lax.rsqrt(var + EPS) * nw_ref[...] + nb_ref[...]).astype(BF16)
    r = lax.broadcasted_iota(jnp.int32, (CHUNK, CHUNK), 0)
    cc = lax.broadcasted_iota(jnp.int32, (CHUNK, CHUNK), 1)
    for g in range(GMLP_GROUPS):
        cols = slice(g * HEAD_DIM, (g + 1) * HEAD_DIM)
        w = jnp.where(r >= cc, ws_ref[g], 0.0).astype(BF16)
        for c in range(ts // CHUNK):
            rows = slice(c * CHUNK, (c + 1) * CHUNK)
            f = jnp.dot(w, v[rows, cols], preferred_element_type=F32) + bs_ref[g]
            o_ref[0, rows, cols] = (u[rows, cols] * f).astype(o_ref.dtype)


def _gmlp(proj, norm_w, norm_b, ws, bs):
    B, S, N = proj.shape
    W = GMLP_GROUPS * HEAD_DIM
    ts = _pick(S, (512, 256, 128))
    half = W // 2
    base = (N - 2 * W) // half
    blk = lambda j: pl.BlockSpec((1, ts, half), lambda b, s, j=j: (b, s, base + j))
    row = pl.BlockSpec((1, W), lambda b, s: (0, 0))
    tab = pl.BlockSpec((GMLP_GROUPS, CHUNK, CHUNK), lambda b, s: (0, 0, 0))
    bs_b = jnp.broadcast_to(bs[:, :, None], (GMLP_GROUPS, CHUNK, HEAD_DIM))
    return pl.pallas_call(
        functools.partial(_gmlp_kernel, ts=ts),
        grid=(B, S // ts),
        in_specs=[blk(0), blk(1), blk(2), blk(3), row, row, tab, tab],
        out_specs=pl.BlockSpec((1, ts, W), lambda b, s: (b, s, 0)),
        out_shape=jax.ShapeDtypeStruct((B, S, W), BF16),
        compiler_params=_params(("parallel", "parallel")),
        name="gmlp",
    )(proj, proj, proj, proj, norm_w.reshape(1, W), norm_b.reshape(1, W), ws, bs_b)


def _outproj_kernel(yr_ref, yd_ref, yg_ref, w_ref, x_ref, mod_ref, nw_ref, rhi_ref, rlo_ref, rb_ref,
                    x1_ref, h2_ref, lg_ref):
    cat = jnp.concatenate([yr_ref[...], yd_ref[...], yg_ref[...]], axis=-1)
    mix = jnp.dot(cat, w_ref[...], preferred_element_type=F32)
    x1 = x_ref[...] + mod_ref[0, 2:3, :] * mix
    x1_ref[...] = x1
    ms = jnp.mean(x1 * x1, axis=-1, keepdims=True)
    h2 = x1 * lax.rsqrt(ms + EPS) * nw_ref[...]
    h2 = h2 * (1.0 + mod_ref[0, 4:5, :]) + mod_ref[0, 3:4, :]
    hi = h2.astype(BF16)
    lo = (h2 - hi.astype(F32)).astype(BF16)
    h2_ref[...] = h2
    rhi = rhi_ref[...]
    lg_ref[...] = (jnp.dot(hi, rhi, preferred_element_type=F32)
                   + jnp.dot(lo, rhi, preferred_element_type=F32)
                   + jnp.dot(hi, rlo_ref[...], preferred_element_type=F32)) + rb_ref[...]


def _out_projection(y_ret, y_diff, y_gm, w_out_bf16, x2d, mod, norm_w, r_hi, r_lo, r_b, seq):
    T, D = x2d.shape
    tm = _pick(seq, (512, 256, 128))
    rowblk = lambda a: pl.BlockSpec((tm, a.shape[1]), lambda i: (i, 0))
    full = lambda a: pl.BlockSpec(a.shape, lambda i: (0, 0))
    nw = norm_w.reshape(1, D)
    return pl.pallas_call(
        _outproj_kernel,
        grid=(T // tm,),
        in_specs=[rowblk(y_ret), rowblk(y_diff), rowblk(y_gm), full(w_out_bf16), rowblk(x2d),
                  pl.BlockSpec((1, N_MOD, D), lambda i: ((i * tm) // seq, 0, 0)),
                  full(nw), full(r_hi), full(r_lo), full(r_b)],
        out_specs=[pl.BlockSpec((tm, D), lambda i: (i, 0)),
                   pl.BlockSpec((tm, D), lambda i: (i, 0)),
                   pl.BlockSpec((tm, ROUTER_PAD), lambda i: (i, 0))],
        out_shape=[jax.ShapeDtypeStruct((T, D), F32),
                   jax.ShapeDtypeStruct((T, D), F32),
                   jax.ShapeDtypeStruct((T, ROUTER_PAD), F32)],
        compiler_params=_params(("parallel",)),
        name="out_projection",
    )(y_ret, y_diff, y_gm, w_out_bf16, x2d, mod, nw, r_hi, r_lo, r_b)


def _row_gather_start(tok_ref, src_hbm, dst_buf, sem, slot, tile_m):
    for r in range(tile_m):
        pltpu.make_async_copy(src_hbm.at[pl.ds(tok_ref[0, 0, r], 1), :],
                              dst_buf.at[slot, pl.ds(r, 1), :], sem.at[slot]).start()


def _moe_up_kernel(te_ref, ti_ref, nv_ref, tokc_ref, tokn_ref, h_hbm, wgu_ref, o_ref,
                   wbf_ref, xbuf_ref, gsem, *, hidden, tile_m):
    i = pl.program_id(0)
    nv = nv_ref[0]

    @pl.when(i < nv)
    def _():
        slot = i % 2

        @pl.when(i == 0)
        def _():
            _row_gather_start(tokc_ref, h_hbm, xbuf_ref, gsem, 0, tile_m)

        @pl.when(i + 1 < nv)
        def _():
            _row_gather_start(tokn_ref, h_hbm, xbuf_ref, gsem, 1 - slot, tile_m)

        @pl.when((i == 0) | (te_ref[i] != te_ref[jnp.maximum(i - 1, 0)]))
        def _():
            wbf_ref[...] = wgu_ref[0, 0].astype(BF16)

        pltpu.make_async_copy(h_hbm.at[pl.ds(0, tile_m), :], xbuf_ref.at[slot], gsem.at[slot]).wait()
        ab = jnp.dot(xbuf_ref[slot].astype(BF16), wbf_ref[...], preferred_element_type=F32)
        o_ref[...] = (jax.nn.silu(ab[:, :hidden]) * ab[:, hidden:]).astype(o_ref.dtype)

    @pl.when(i >= nv)
    def _():
        o_ref[...] = jnp.zeros_like(o_ref)


def _moe_down_kernel(te_ref, ti_ref, nv_ref, dst_ref, hm_ref, wdn_ref, g_ref, y_hbm,
                     wbf_ref, ybuf_ref, ssem, *, tile_m, y_rows):
    i = pl.program_id(0)
    nv = nv_ref[0]

    def scatter_wait(slot):
        pltpu.make_async_copy(ybuf_ref.at[slot], y_hbm.at[pl.ds(0, tile_m), :], ssem.at[slot]).wait()

    @pl.when(i == 0)
    def _():
        ybuf_ref[...] = jnp.zeros_like(ybuf_ref)
        spare0 = y_rows - 2 * tile_m
        fills = [pltpu.make_async_copy(ybuf_ref.at[s], y_hbm.at[pl.ds(k * y_rows + spare0 + s * tile_m, tile_m), :],
                                       ssem.at[s]) for k in range(TOP_K) for s in range(2)]
        for cp in fills:
            cp.start()
        for cp in fills:
            cp.wait()

    @pl.when(i < nv)
    def _():
        slot = i % 2

        @pl.when((i == 0) | (te_ref[i] != te_ref[jnp.maximum(i - 1, 0)]))
        def _():
            wbf_ref[...] = wdn_ref[0, 0].astype(BF16)

        @pl.when(i >= 2)
        def _():
            scatter_wait(slot)

        ybuf_ref[slot] = jnp.dot(hm_ref[...], wbf_ref[...], preferred_element_type=F32) * g_ref[...]
        for r in range(tile_m):
            pltpu.make_async_copy(ybuf_ref.at[slot, pl.ds(r, 1), :],
                                  y_hbm.at[pl.ds(dst_ref[0, 0, r], 1), :], ssem.at[slot]).start()

        @pl.when(i == nv - 1)
        def _():
            scatter_wait(slot)

            @pl.when(i >= 1)
            def _():
                scatter_wait(1 - slot)


def _expert_ffn(h2, slot_tok, slot_dst, slot_gate, tile_exp, tile_idx, n_valid, w_gu, w_dn, layer, tile_m, y_rows):
    T, D = h2.shape
    hidden = w_dn.shape[2]
    n_tiles = tile_exp.shape[0]
    P = n_tiles * tile_m
    tok3 = slot_tok.reshape(n_tiles, 1, tile_m)
    dst3 = slot_dst.reshape(n_tiles, 1, tile_m)
    smem_blk = lambda f: pl.BlockSpec((1, 1, tile_m), f, memory_space=pltpu.SMEM)
    hmid = pl.pallas_call(
        functools.partial(_moe_up_kernel, hidden=hidden, tile_m=tile_m),
        grid_spec=pltpu.PrefetchScalarGridSpec(
            num_scalar_prefetch=3,
            grid=(n_tiles,),
            in_specs=[smem_blk(lambda i, te, ti, nv: (i, 0, 0)),
                      smem_blk(lambda i, te, ti, nv: (jnp.minimum(i + 1, n_tiles - 1), 0, 0)),
                      pl.BlockSpec(memory_space=pl.ANY),
                      pl.BlockSpec((1, 1, D, 2 * hidden), lambda i, te, ti, nv: (layer, te[i], 0, 0))],
            out_specs=pl.BlockSpec((tile_m, hidden), lambda i, te, ti, nv: (i, 0)),
            scratch_shapes=[pltpu.VMEM((D, 2 * hidden), BF16),
                            pltpu.VMEM((2, tile_m, D), F32),
                            pltpu.SemaphoreType.DMA((2,))]),
        out_shape=jax.ShapeDtypeStruct((P, hidden), BF16),
        compiler_params=_params(("arbitrary",)),
        name="expert_up",
    )(tile_exp, tile_idx, n_valid, tok3, tok3, h2, w_gu)
    return pl.pallas_call(
        functools.partial(_moe_down_kernel, tile_m=tile_m, y_rows=y_rows),
        grid_spec=pltpu.PrefetchScalarGridSpec(
            num_scalar_prefetch=3,
            grid=(n_tiles,),
            in_specs=[smem_blk(lambda i, te, ti, nv: (i, 0, 0)),
                      pl.BlockSpec((tile_m, hidden), lambda i, te, ti, nv: (ti[i], 0)),
                      pl.BlockSpec((1, 1, hidden, D), lambda i, te, ti, nv: (layer, te[i], 0, 0)),
                      pl.BlockSpec((tile_m, 1), lambda i, te, ti, nv: (ti[i], 0))],
            out_specs=pl.BlockSpec(memory_space=pl.ANY),
            scratch_shapes=[pltpu.VMEM((hidden, D), BF16),
                            pltpu.VMEM((2, tile_m, D), F32),
                            pltpu.SemaphoreType.DMA((2,))]),
        out_shape=jax.ShapeDtypeStruct((TOP_K * y_rows, D), F32),
        compiler_params=_params(("arbitrary",)),
        name="expert_down",
    )(tile_exp, tile_idx, n_valid, dst3, hmid, w_dn, slot_gate.reshape(P, 1))


def _combine_kernel(x_ref, ya_ref, yb_ref, mod_ref, o_ref):
    o_ref[...] = x_ref[...] + mod_ref[0, 5:6, :] * (ya_ref[0] + yb_ref[0])


def _moe_combine(x1, y2, mod, seq):
    T, D = x1.shape
    tm = _pick(seq, (512, 256, 128))
    row = pl.BlockSpec((tm, D), lambda i: (i, 0))
    return pl.pallas_call(
        _combine_kernel,
        grid=(T // tm,),
        in_specs=[row,
                  pl.BlockSpec((1, tm, D), lambda i: (0, i, 0)),
                  pl.BlockSpec((1, tm, D), lambda i: (1, i, 0)),
                  pl.BlockSpec((1, N_MOD, D), lambda i: ((i * tm) // seq, 0, 0))],
        out_specs=row,
        out_shape=jax.ShapeDtypeStruct((T, D), F32),
        compiler_params=_params(("parallel",)),
        name="moe_combine",
    )(x1, y2, y2, mod)


def _route(logits, tile_m, y_rows):
    T = logits.shape[0]
    gl = logits[:, :N_GROUPS]
    el = logits[:, N_GROUPS:N_GROUPS + N_EXPERTS].reshape(T, N_GROUPS, EXPERTS_PER_GROUP)
    g_idx = jnp.argmax(gl, axis=-1)
    p_g = jnp.take_along_axis(jax.nn.softmax(gl, axis=-1), g_idx[:, None], axis=-1)
    el = jnp.take_along_axis(el, g_idx[:, None, None], axis=1)[:, 0]
    top_v, top_i = lax.top_k(el, TOP_K)
    gate = (p_g * jax.nn.softmax(top_v, axis=-1)).reshape(-1)
    eid = (g_idx[:, None] * EXPERTS_PER_GROUP + top_i).reshape(-1).astype(jnp.int32)
    A = T * TOP_K
    order = jnp.argsort(eid).astype(jnp.int32)
    counts = jnp.sum((eid[:, None] == jnp.arange(N_EXPERTS, dtype=jnp.int32)[None, :]).astype(jnp.int32), axis=0)
    starts = jnp.cumsum(counts) - counts
    ptiles = (counts + tile_m - 1) // tile_m
    pends = jnp.cumsum(ptiles)
    pstarts = pends - ptiles
    n_tiles = -(-A // tile_m) + N_EXPERTS
    n_valid = pends[-1].astype(jnp.int32)
    tiles = jnp.arange(n_tiles, dtype=jnp.int32)
    tile_idx = jnp.minimum(tiles, n_valid - 1)
    tile_exp = jnp.minimum(jnp.sum((pends[None, :] <= tile_idx[:, None]).astype(jnp.int32), axis=1), N_EXPERTS - 1)
    slot = jnp.arange(n_tiles * tile_m, dtype=jnp.int32)
    s_tile = slot // tile_m
    s_exp = tile_exp[s_tile]
    s_rank = slot - pstarts[s_exp] * tile_m
    s_ok = (s_tile < n_valid) & (s_rank < counts[s_exp])
    s_asg = order[jnp.clip(starts[s_exp] + s_rank, 0, A - 1)]
    slot_tok = jnp.where(s_ok, s_asg // TOP_K, 0)
    spare = T + (s_tile % 2) * tile_m + slot % tile_m
    slot_dst = jnp.where(s_ok, (s_asg % TOP_K) * y_rows + s_asg // TOP_K, spare)
    slot_gate = jnp.where(s_ok, gate[s_asg], 0.0)
    return slot_tok, slot_dst, slot_gate, tile_exp, tile_idx, n_valid.reshape(1)


def kernel(x, c, ada_w, ada_b, mix_norm_w, w_in, w_out, ret_norm_w, diff_q_norm_w, diff_k_norm_w, diff_lambda, diff_subln_w, gmlp_norm_w, gmlp_norm_b, gmlp_ws, gmlp_bs, ffn_norm_w, router_group_w, router_group_b, router_expert_w, router_expert_b, expert_w_gate_up, expert_w_down):
    B, S, D = x.shape
    L = ada_w.shape[0]
    T = B * S
    tile_m = 256
    y_rows = T + 2 * tile_m
    mod_all =_modulation(c, ada_w, ada_b).reshape(L, B, N_MOD, D)
    x2d = x.reshape(T, D)
    for l in range(L):
        mod = mod_all[l]
        proj = _in_projection(x2d, mod, mix_norm_w[l], w_in[l].astype(BF16), S).reshape(B, S, -1)
        lam_init = 0.8 - 0.6 * math.exp(-0.3 * l)
        y_ret = _retention(proj, ret_norm_w[l])
        y_diff = _diff_attention(proj, diff_q_norm_w[l], diff_k_norm_w[l], diff_lambda[l],
                                 diff_subln_w[l], lam_init)
        y_gm = _gmlp(proj, gmlp_norm_w[l], gmlp_norm_b[l], gmlp_ws[l], gmlp_bs[l])

        r_w = jnp.concatenate([router_group_w[l], router_expert_w[l]], axis=1)
        r_w = jnp.pad(r_w, ((0, 0), (0, ROUTER_PAD - r_w.shape[1])))
        r_hi = r_w.astype(BF16)
        r_lo = (r_w - r_hi.astype(F32)).astype(BF16)
        r_b = jnp.pad(jnp.concatenate([router_group_b[l], router_expert_b[l]]),
                      (0, ROUTER_PAD - N_GROUPS - N_EXPERTS)).reshape(1, ROUTER_PAD)
        x1, h2, logits = _out_projection(
            y_ret.reshape(T, -1), y_diff.reshape(T, -1), y_gm.reshape(T, -1), w_out[l].astype(BF16),
            x2d, mod, ffn_norm_w[l], r_hi, r_lo, r_b, S)

        slot_tok, slot_dst, slot_gate, tile_exp, tile_idx, n_valid = _route(logits, tile_m, y_rows)
        y2 = _expert_ffn(h2, slot_tok, slot_dst, slot_gate, tile_exp, tile_idx, n_valid,
                         expert_w_gate_up, expert_w_down, l, tile_m, y_rows)
        x2d = _moe_combine(x1, y2.reshape(TOP_K, y_rows, D), mod, S)
    return x2d.reshape(B, S, D)
```
